```python
import math
import jax, jax.numpy as jnp
from jax import lax
import numpy as np

D_MODEL = 4096
BATCH = 1
SEQ = 8192
DEPTH = 1
DEC_BATCH = 128
DEC_SEQ = 8
PAST_LEN = 2048
PAGE_SIZE = 128

ATTN_WIDTH = D_MODEL // 2
SSM_WIDTH = D_MODEL - ATTN_WIDTH
HEAD_DIM = 128
N_HEADS = ATTN_WIDTH // HEAD_DIM
MOBA_BLOCK = 256
MOBA_TOPK = 3
Q_CHUNK = 32
S5_GROUP = 16
S5_GROUPS = SSM_WIDTH // S5_GROUP
S5_STATE = 64
D_FF = -(-8 * D_MODEL // (3 * 256)) * 256
N_MIX = 3 * ATTN_WIDTH + SSM_WIDTH
EPS = 1e-6
NEG = -1e30

kernel_name = 'hymba_moba_s5_adaln_step'


def _rmsnorm(x, g):
    xf = x.astype(jnp.float32)
    r = lax.rsqrt(jnp.mean(xf * xf, axis=-1, keepdims=True) + EPS)
    return (xf * r).astype(x.dtype) * g


def _adaln(c, w_ada, b_ada):
    mod = (jax.nn.silu(c) @ w_ada + b_ada)[:, None, :]
    return jnp.split(mod, 6, axis=-1)


def _select_blocks(q, k_means, n_avail, n_sel):
    s = jnp.einsum('qhd,nhd->qhn', q.astype(jnp.float32), k_means.astype(jnp.float32))
    nc = k_means.shape[0]
    s = jnp.where(jnp.arange(nc)[None, None, :] < n_avail[:, None, None], s, -jnp.inf)
    _, idx = lax.top_k(s, n_sel)
    valid = jnp.arange(n_sel)[None, None, :] < n_avail[:, None, None]
    return idx, valid


def _attend(q, k_own, v_own, own_mask, k_sel=None, v_sel=None, sel_valid=None):
    qf = q.astype(jnp.float32) * (HEAD_DIM ** -0.5)
    lo = jnp.einsum('qhd,nhd->qhn', qf, k_own.astype(jnp.float32))
    lo = jnp.where(own_mask[:, None, :], lo, NEG)
    if k_sel is None:
        p = jax.nn.softmax(lo, axis=-1)
        out = jnp.einsum('qhn,nhd->qhd', p, v_own.astype(jnp.float32))
        return out.astype(q.dtype)
    ls = jnp.einsum('qhd,qhjld->qhjl', qf, k_sel.astype(jnp.float32))
    ls = jnp.where(sel_valid[..., None], ls, NEG)
    nq, nh, nj, nl = ls.shape
    p = jax.nn.softmax(jnp.concatenate([ls.reshape(nq, nh, nj * nl), lo], axis=-1), axis=-1)
    ps = p[..., :nj * nl].reshape(nq, nh, nj, nl)
    po = p[..., nj * nl:]
    out = (jnp.einsum('qhjl,qhjld->qhd', ps, v_sel.astype(jnp.float32))
           + jnp.einsum('qhn,nhd->qhd', po, v_own.astype(jnp.float32)))
    return out.astype(q.dtype)


def _moba_prompt(q, k, v):
    s_len = q.shape[0]
    n_cand = (s_len - 1) // MOBA_BLOCK
    n_sel = min(MOBA_TOPK, n_cand)
    n_blk = -(-s_len // MOBA_BLOCK)
    pad = n_blk * MOBA_BLOCK - s_len
    kb = jnp.pad(k, ((0, pad), (0, 0), (0, 0))).reshape(n_blk, MOBA_BLOCK, N_HEADS, HEAD_DIM)
    vb = jnp.pad(v, ((0, pad), (0, 0), (0, 0))).reshape(n_blk, MOBA_BLOCK, N_HEADS, HEAD_DIM)
    hidx = jnp.arange(N_HEADS)[None, :, None]
    if n_sel > 0:
        k_means = jnp.mean(kb[:n_cand].astype(jnp.float32), axis=1)
        idx, valid = _select_blocks(q, k_means, jnp.arange(s_len) // MOBA_BLOCK, n_sel)

    def chunk_fn(ci):
        start = ci * Q_CHUNK
        qc = lax.dynamic_slice_in_dim(q, start, Q_CHUNK, 0)
        qpos = start + jnp.arange(Q_CHUNK)
        ob = start // MOBA_BLOCK
        k_own = lax.dynamic_index_in_dim(kb, ob, 0, keepdims=False)
        v_own = lax.dynamic_index_in_dim(vb, ob, 0, keepdims=False)
        kpos = ob * MOBA_BLOCK + jnp.arange(MOBA_BLOCK)
        own_mask = kpos[None, :] <= qpos[:, None]
        if n_sel == 0:
            return _attend(qc, k_own, v_own, own_mask)
        ic = lax.dynamic_slice_in_dim(idx, start, Q_CHUNK, 0)
        vc = lax.dynamic_slice_in_dim(valid, start, Q_CHUNK, 0)
        k_sel = kb[ic, :, hidx, :]
        v_sel = vb[ic, :, hidx, :]
        return _attend(qc, k_own, v_own, own_mask, k_sel, v_sel, vc)

    out = lax.map(chunk_fn, jnp.arange(s_len // Q_CHUNK))
    return out.reshape(s_len, N_HEADS, HEAD_DIM)


def _moba_sample(q, k_new, v_new, pt, ck, cv):
    t = q.shape[0]
    ppb = MOBA_BLOCK // PAGE_SIZE
    n_past_blk = PAST_LEN // MOBA_BLOCK
    n_sel = min(MOBA_TOPK, n_past_blk)
    qpos = PAST_LEN + jnp.arange(t)
    r_rows = PAST_LEN - n_past_blk * MOBA_BLOCK
    k_own, v_own, kpos = k_new, v_new, qpos
    if r_rows > 0:
        own_pages = pt[n_past_blk * ppb: n_past_blk * ppb + r_rows // PAGE_SIZE]
        k_own = jnp.concatenate([ck[own_pages].reshape(r_rows, N_HEADS, HEAD_DIM), k_new], axis=0)
        v_own = jnp.concatenate([cv[own_pages].reshape(r_rows, N_HEADS, HEAD_DIM), v_new], axis=0)
        kpos = jnp.concatenate([n_past_blk * MOBA_BLOCK + jnp.arange(r_rows), qpos])
    own_mask = ((kpos[None, :] <= qpos[:, None])
                & (kpos[None, :] // MOBA_BLOCK == qpos[:, None] // MOBA_BLOCK))
    if n_sel == 0:
        return _attend(q, k_own, v_own, own_mask)
    k_past = ck[pt[:n_past_blk * ppb]].reshape(n_past_blk, MOBA_BLOCK, N_HEADS, HEAD_DIM)
    k_means = jnp.mean(k_past.astype(jnp.float32), axis=1)
    n_avail = jnp.minimum(qpos // MOBA_BLOCK, n_past_blk)
    idx, valid = _select_blocks(q, k_means, n_avail, n_sel)
    hidx = jnp.arange(N_HEADS)[None, :, None]
    k_sel = k_past[idx, :, hidx, :]
    phys = pt[idx[..., None] * ppb + jnp.arange(ppb)]
    v_sel = cv[phys, :, hidx[..., None], :].reshape(t, N_HEADS, n_sel, MOBA_BLOCK, HEAD_DIM)
    return _attend(q, k_own, v_own, own_mask, k_sel, v_sel, valid)


def _s5(u, lam_re, lam_im, log_step, b_re, b_im, c_re, c_im, d_skip, w_glu, b_glu, h0_re, h0_im):
    n, t, _ = u.shape
    uf = u.astype(jnp.float32)
    step = jnp.exp(log_step.astype(jnp.float32))[:, None]
    lr = lam_re.astype(jnp.float32)
    li = lam_im.astype(jnp.float32)
    mag = jnp.exp(lr * step)
    a_re = mag * jnp.cos(li * step)
    a_im = mag * jnp.sin(li * step)
    den = lr * lr + li * li
    f_re = ((a_re - 1.0) * lr + a_im * li) / den
    f_im = (a_im * lr - (a_re - 1.0) * li) / den
    br = b_re.astype(jnp.float32)
    bi = b_im.astype(jnp.float32)
    bb_re = f_re[..., None] * br - f_im[..., None] * bi
    bb_im = f_re[..., None] * bi + f_im[..., None] * br
    ug = uf.reshape(n, t, S5_GROUPS, S5_GROUP)
    x_re = jnp.einsum('gpc,ntgc->ntgp', bb_re, ug)
    x_im = jnp.einsum('gpc,ntgc->ntgp', bb_im, ug)
    if h0_re is not None:
        hr0 = h0_re.astype(jnp.float32)
        hi0 = h0_im.astype(jnp.float32)
        x_re = x_re.at[:, 0].add(a_re * hr0 - a_im * hi0)
        x_im = x_im.at[:, 0].add(a_re * hi0 + a_im * hr0)
    ar = jnp.broadcast_to(a_re, x_re.shape)
    ai = jnp.broadcast_to(a_im, x_im.shape)

    def combine(e1, e2):
        a1r, a1i, b1r, b1i = e1
        a2r, a2i, b2r, b2i = e2
        return (a2r * a1r - a2i * a1i, a2r * a1i + a2i * a1r,
                a2r * b1r - a2i * b1i + b2r, a2r * b1i + a2i * b1r + b2i)

    _, _, h_re, h_im = lax.associative_scan(combine, (ar, ai, x_re, x_im), axis=1)
    y = (jnp.einsum('gcp,ntgp->ntgc', c_re.astype(jnp.float32), h_re)
         - jnp.einsum('gcp,ntgp->ntgc', c_im.astype(jnp.float32), h_im))
    y = y.reshape(n, t, SSM_WIDTH) + d_skip.astype(jnp.float32) * uf
    z = jax.nn.gelu(y)
    out = z * jax.nn.sigmoid(z @ w_glu.astype(jnp.float32) + b_glu.astype(jnp.float32))
    return out.astype(u.dtype), h_re[:, -1], h_im[:, -1]


def _block(x, c, attend, h0_re, h0_im, w_ada, b_ada, g_norm1, w_in, g_q, g_k,
           s5_lam_re, s5_lam_im, s5_log_step, s5_b_re, s5_b_im, s5_c_re, s5_c_im, s5_d,
           w_glu, b_glu, g_attn_out, g_ssm_out, w_out, g_norm2, w_gate, w_up, w_down):
    n, t, _ = x.shape
    sh_a, sc_a, gt_a, sh_f, sc_f, gt_f = _adaln(c, w_ada, b_ada)
    h = _rmsnorm(x, g_norm1) * (1.0 + sc_a) + sh_a
    q, k, v, u = jnp.split(h @ w_in, [ATTN_WIDTH, 2 * ATTN_WIDTH, 3 * ATTN_WIDTH], axis=-1)
    heads = (n, t, N_HEADS, HEAD_DIM)
    q = _rmsnorm(q.reshape(heads), g_q)
    k = _rmsnorm(k.reshape(heads), g_k)
    v = v.reshape(heads)
    o_attn = attend(q, k, v).reshape(n, t, ATTN_WIDTH)
    o_ssm, h_re, h_im = _s5(u, s5_lam_re, s5_lam_im, s5_log_step, s5_b_re, s5_b_im,
                            s5_c_re, s5_c_im, s5_d, w_glu, b_glu, h0_re, h0_im)
    mixed = jnp.concatenate([_rmsnorm(o_attn, g_attn_out), _rmsnorm(o_ssm, g_ssm_out)], axis=-1) @ w_out
    x = x + gt_a * mixed
    h2 = _rmsnorm(x, g_norm2) * (1.0 + sc_f) + sh_f
    x = x + gt_f * ((jax.nn.silu(h2 @ w_gate) * (h2 @ w_up)) @ w_down)
    return x, k, v, h_re, h_im


def setup_inputs(seed: int = 0) -> dict:
    key = jax.random.key(seed)
    ks = jax.random.split(key, 32)
    f32 = jnp.float32
    n_pages = PAST_LEN // PAGE_SIZE
    n_used = DEC_BATCH * n_pages
    n_pool = n_used + max(1, n_used // 4)

    def nrm(k, shape, scale):
        return jax.random.normal(k, shape, f32) * scale

    def gain(k, shape):
        return 1.0 + 0.02 * jax.random.normal(k, shape, f32)

    pool_shape = (DEPTH, n_pool, PAGE_SIZE, N_HEADS, HEAD_DIM)
    state_shape = (DEPTH, DEC_BATCH, S5_GROUPS, S5_STATE)
    s5_shape = (DEPTH, S5_GROUPS, S5_STATE)
    page_table = jax.random.permutation(ks[6], n_pool)[:n_used].reshape(DEC_BATCH, n_pages).astype(jnp.int32)
    lam_im0 = math.pi * jnp.arange(S5_STATE, dtype=f32)
    return {
        'x_prompt': nrm(ks[0], (BATCH, SEQ, D_MODEL), 1.0),
        'x_sample': nrm(ks[1], (DEC_BATCH, DEC_SEQ, D_MODEL), 1.0),
        'cache_k': nrm(ks[2], pool_shape, 1.0),
        'cache_v': nrm(ks[3], pool_shape, 1.0),
        'state_s5_re': nrm(ks[4], state_shape, 0.5),
        'state_s5_im': nrm(ks[5], state_shape, 0.5),
        'page_table': page_table,
        'c_prompt': nrm(ks[7], (BATCH, D_MODEL), 1.0),
        'c_sample': nrm(ks[8], (DEC_BATCH, D_MODEL), 1.0),
        'w_ada': nrm(ks[9], (DEPTH, D_MODEL, 6 * D_MODEL), D_MODEL ** -0.5),
        'b_ada': nrm(ks[10], (DEPTH, 6 * D_MODEL), 0.02),
        'g_norm1': gain(ks[11], (DEPTH, D_MODEL)),
        'w_in': nrm(ks[12], (DEPTH, D_MODEL, N_MIX), D_MODEL ** -0.5),
        'g_q': gain(ks[13], (DEPTH, HEAD_DIM)),
        'g_k': gain(ks[14], (DEPTH, HEAD_DIM)),
        's5_lam_re': -0.5 + nrm(ks[15], s5_shape, 0.01),
        's5_lam_im': lam_im0 + nrm(ks[16], s5_shape, 0.01),
        's5_log_step': jax.random.uniform(ks[17], (DEPTH, S5_GROUPS), f32, math.log(1e-3), math.log(1e-1)),
        's5_b_re': nrm(ks[18], (DEPTH, S5_GROUPS, S5_STATE, S5_GROUP), (2 * S5_GROUP) ** -0.5),
        's5_b_im': nrm(ks[19], (DEPTH, S5_GROUPS, S5_STATE, S5_GROUP), (2 * S5_GROUP) ** -0.5),
        's5_c_re': nrm(ks[20], (DEPTH, S5_GROUPS, S5_GROUP, S5_STATE), (2 * S5_STATE) ** -0.5),
        's5_c_im': nrm(ks[21], (DEPTH, S5_GROUPS, S5_GROUP, S5_STATE), (2 * S5_STATE) ** -0.5),
        's5_d': nrm(ks[22], (DEPTH, SSM_WIDTH), 1.0),
        'w_glu': nrm(ks[23], (DEPTH, SSM_WIDTH, SSM_WIDTH), SSM_WIDTH ** -0.5),
        'b_glu': nrm(ks[24], (DEPTH, SSM_WIDTH), 0.02),
        'g_attn_out': gain(ks[25], (DEPTH, ATTN_WIDTH)),
        'g_ssm_out': gain(ks[26], (DEPTH, SSM_WIDTH)),
        'w_out': nrm(ks[27], (DEPTH, D_MODEL, D_MODEL), D_MODEL ** -0.5),
        'g_norm2': gain(ks[28], (DEPTH, D_MODEL)),
        'w_gate': nrm(ks[29], (DEPTH, D_MODEL, D_FF), D_MODEL ** -0.5),
        'w_up': nrm(ks[30], (DEPTH, D_MODEL, D_FF), D_MODEL ** -0.5),
        'w_down': nrm(ks[31], (DEPTH, D_FF, D_MODEL), D_FF ** -0.5),
    }


def reference(x_prompt, x_sample, cache_k, cache_v, state_s5_re, state_s5_im, page_table,
              c_prompt, c_sample, w_ada, b_ada, g_norm1, w_in, g_q, g_k,
              s5_lam_re, s5_lam_im, s5_log_step, s5_b_re, s5_b_im, s5_c_re, s5_c_im, s5_d,
              w_glu, b_glu, g_attn_out, g_ssm_out, w_out, g_norm2, w_gate, w_up, w_down):
    y_p, y_s = x_prompt, x_sample
    kp_list, vp_list, hrp_list, hip_list = [], [], [], []
    ks_list, vs_list, hrs_list, his_list = [], [], [], []
    for l in range(DEPTH):
        lw = dict(w_ada=w_ada[l], b_ada=b_ada[l], g_norm1=g_norm1[l], w_in=w_in[l],
                  g_q=g_q[l], g_k=g_k[l], s5_lam_re=s5_lam_re[l], s5_lam_im=s5_lam_im[l],
                  s5_log_step=s5_log_step[l], s5_b_re=s5_b_re[l], s5_b_im=s5_b_im[l],
                  s5_c_re=s5_c_re[l], s5_c_im=s5_c_im[l], s5_d=s5_d[l], w_glu=w_glu[l],
                  b_glu=b_glu[l], g_attn_out=g_attn_out[l], g_ssm_out=g_ssm_out[l],
                  w_out=w_out[l], g_norm2=g_norm2[l], w_gate=w_gate[l], w_up=w_up[l],
                  w_down=w_down[l])
        ck = cache_k[l]
        cv = cache_v[l]

        def attend_sample(q, k, v, ck=ck, cv=cv):
            return lax.map(lambda a: _moba_sample(a[0], a[1], a[2], a[3], ck, cv),
                           (q, k, v, page_table))

        y_p, kp, vp, hrp, hip = _block(y_p, c_prompt, jax.vmap(_moba_prompt), None, None, **lw)
        y_s, ks_new, vs_new, hrs, his = _block(y_s, c_sample, attend_sample,
                                               state_s5_re[l], state_s5_im[l], **lw)
        kp_list.append(kp)
        vp_list.append(vp)
        hrp_list.append(hrp)
        hip_list.append(hip)
        ks_list.append(ks_new)
        vs_list.append(vs_new)
        hrs_list.append(hrs)
        his_list.append(his)
    return (y_p, y_s, jnp.stack(kp_list), jnp.stack(vp_list), jnp.stack(hrp_list), jnp.stack(hip_list),
            jnp.stack(ks_list), jnp.stack(vs_list), jnp.stack(hrs_list), jnp.stack(his_list))
```

```python
import functools
import math

import jax
import jax.numpy as jnp
from jax import lax
from jax.experimental import pallas as pl
from jax.experimental.pallas import tpu as pltpu

F32 = jnp.float32
BF16 = jnp.bfloat16

HEAD_DIM = 128
MOBA_BLOCK = 256
MOBA_TOPK = 3
PAGE_SIZE = 128
S5_GROUP = 16
S5_STATE = 64
S5_CHUNK = 8
EPS = 1e-6
NEG = -1e30

LANES = 128
VMEM_LIMIT = 56 * 1024 * 1024


def _cparams(*sem):
    return pltpu.CompilerParams(dimension_semantics=sem, vmem_limit_bytes=VMEM_LIMIT)


def _adaln_body(c_ref, w_ref, b_ref, o_ref):
    c = c_ref[...]
    s = (c * jax.nn.sigmoid(c)).astype(BF16)
    o_ref[...] = jnp.dot(s, w_ref[...].astype(BF16), preferred_element_type=F32) + b_ref[...]


def adaln(c, w_ada, b_ada, tn=512):
    m, d = c.shape
    n = w_ada.shape[1]
    return pl.pallas_call(
        _adaln_body,
        grid=(n // tn,),
        in_specs=[pl.BlockSpec((m, d), lambda j: (0, 0)),
                  pl.BlockSpec((d, tn), lambda j: (0, j)),
                  pl.BlockSpec((1, tn), lambda j: (0, j))],
        out_specs=pl.BlockSpec((m, tn), lambda j: (0, j)),
        out_shape=jax.ShapeDtypeStruct((m, n), F32),
        compiler_params=_cparams("arbitrary"),
        name="adaln",
    )(c, w_ada, b_ada.reshape(1, n))


def _mod_spec(mod, tm, tn, chunk, n_col_blocks):
    if mod.shape[0] == 1:
        return pl.BlockSpec((1, tn), lambda i, j: (0, chunk * n_col_blocks + j))
    return pl.BlockSpec((tm, tn), lambda i, j: (i, chunk * n_col_blocks + j))


def _norm_mod_body(x_ref, g_ref, sc_ref, sh_ref, o_ref):
    x = x_ref[...]
    r = lax.rsqrt(jnp.mean(x * x, axis=-1, keepdims=True) + EPS)
    h = ((x * r) * g_ref[...]) * (1.0 + sc_ref[...]) + sh_ref[...]
    o_ref[...] = h.astype(BF16)


def norm_mod(x, g, mod, sc_chunk, sh_chunk, tm=256):
    t, d = x.shape
    tm = min(tm, t)
    return pl.pallas_call(
        _norm_mod_body,
        grid=(t // tm, 1),
        in_specs=[pl.BlockSpec((tm, d), lambda i, j: (i, 0)),
                  pl.BlockSpec((1, d), lambda i, j: (0, 0)),
                  _mod_spec(mod, tm, d, sc_chunk, 1),
                  _mod_spec(mod, tm, d, sh_chunk, 1)],
        out_specs=pl.BlockSpec((tm, d), lambda i, j: (i, 0)),
        out_shape=jax.ShapeDtypeStruct((t, d), BF16),
        compiler_params=_cparams("arbitrary", "arbitrary"),
        name="norm_mod",
    )(x, g.reshape(1, d), mod, mod)


def _inproj_body(h_ref, w_ref, gq_ref, gk_ref, o_ref, *, tn, attn_width):
    j = pl.program_id(1)
    acc = jnp.dot(h_ref[...], w_ref[...], preferred_element_type=F32)
    col0 = j * tn

    @pl.when(col0 < 2 * attn_width)
    def _():
        g = jnp.where(col0 < attn_width, gq_ref[...], gk_ref[...])
        for s in range(tn // HEAD_DIM):
            a = acc[:, s * HEAD_DIM:(s + 1) * HEAD_DIM]
            r = lax.rsqrt(jnp.mean(a * a, axis=-1, keepdims=True) + EPS)
            o_ref[:, s * HEAD_DIM:(s + 1) * HEAD_DIM] = (a * r) * g

    @pl.when(col0 >= 2 * attn_width)
    def _():
        o_ref[...] = acc


def inproj(h, w_in, g_q, g_k, attn_width, tm=1024, tn=512):
    t, d = h.shape
    n = w_in.shape[1]
    tm = min(tm, t)
    return pl.pallas_call(
        functools.partial(_inproj_body, tn=tn, attn_width=attn_width),
        grid=(t // tm, n // tn),
        in_specs=[pl.BlockSpec((tm, d), lambda i, j: (i, 0)),
                  pl.BlockSpec((d, tn), lambda i, j: (0, j)),
                  pl.BlockSpec((1, HEAD_DIM), lambda i, j: (0, 0)),
                  pl.BlockSpec((1, HEAD_DIM), lambda i, j: (0, 0))],
        out_specs=pl.BlockSpec((tm, tn), lambda i, j: (i, j)),
        out_shape=jax.ShapeDtypeStruct((t, n), F32),
        compiler_params=_cparams("arbitrary", "arbitrary"),
        name="inproj",
    )(h, w_in, g_q.reshape(1, HEAD_DIM), g_k.reshape(1, HEAD_DIM))


def _matmul_res_body(a_ref, w_ref, res_ref, gt_ref, o_ref):
    acc = jnp.dot(a_ref[...], w_ref[...], preferred_element_type=F32)
    o_ref[...] = res_ref[...] + gt_ref[...] * acc


def matmul_res(a, w, res, mod, gt_chunk, tm, tn):
    t, k = a.shape
    n = w.shape[1]
    tm = min(tm, t)
    return pl.pallas_call(
        _matmul_res_body,
        grid=(t // tm, n // tn),
        in_specs=[pl.BlockSpec((tm, k), lambda i, j: (i, 0)),
                  pl.BlockSpec((k, tn), lambda i, j: (0, j)),
                  pl.BlockSpec((tm, tn), lambda i, j: (i, j)),
                  _mod_spec(mod, tm, tn, gt_chunk, n // tn)],
        out_specs=pl.BlockSpec((tm, tn), lambda i, j: (i, j)),
        out_shape=jax.ShapeDtypeStruct((t, n), F32),
        compiler_params=_cparams("arbitrary", "arbitrary"),
        name="matmul_res",
    )(a, w, res, mod)


def _gateup_body(h_ref, wg_ref, wu_ref, o_ref):
    h = h_ref[...]
    g = jnp.dot(h, wg_ref[...], preferred_element_type=F32)
    u = jnp.dot(h, wu_ref[...], preferred_element_type=F32)
    o_ref[...] = ((g * jax.nn.sigmoid(g)) * u).astype(BF16)


def gateup(h, w_gate, w_up, tm=1024, tn=256):
    t, d = h.shape
    n = w_gate.shape[1]
    tm = min(tm, t)
    return pl.pallas_call(
        _gateup_body,
        grid=(t // tm, n // tn),
        in_specs=[pl.BlockSpec((tm, d), lambda i, j: (i, 0)),
                  pl.BlockSpec((d, tn), lambda i, j: (0, j)),
                  pl.BlockSpec((d, tn), lambda i, j: (0, j))],
        out_specs=pl.BlockSpec((tm, tn), lambda i, j: (i, j)),
        out_shape=jax.ShapeDtypeStruct((t, n), BF16),
        compiler_params=_cparams("arbitrary", "arbitrary"),
        name="gateup",
    )(h, w_gate, w_up)


def _rms2_body(a_ref, s_ref, ga_ref, gs_ref, o_ref, *, wa):
    a = a_ref[...]
    ra = lax.rsqrt(jnp.mean(a * a, axis=-1, keepdims=True) + EPS)
    o_ref[:, :wa] = ((a * ra) * ga_ref[...]).astype(BF16)
    s = s_ref[...]
    rs = lax.rsqrt(jnp.mean(s * s, axis=-1, keepdims=True) + EPS)
    o_ref[:, wa:] = ((s * rs) * gs_ref[...]).astype(BF16)


def rms2(o_attn, o_ssm, g_a, g_s, tm=256):
    t, wa = o_attn.shape
    ws = o_ssm.shape[1]
    tm = min(tm, t)
    return pl.pallas_call(
        functools.partial(_rms2_body, wa=wa),
        grid=(t // tm,),
        in_specs=[pl.BlockSpec((tm, wa), lambda i: (i, 0)),
                  pl.BlockSpec((tm, ws), lambda i: (i, 0)),
                  pl.BlockSpec((1, wa), lambda i: (0, 0)),
                  pl.BlockSpec((1, ws), lambda i: (0, 0))],
        out_specs=pl.BlockSpec((tm, wa + ws), lambda i: (i, 0)),
        out_shape=jax.ShapeDtypeStruct((t, wa + ws), BF16),
        compiler_params=_cparams("arbitrary"),
        name="rms2",
    )(o_attn, o_ssm, g_a.reshape(1, wa), g_s.reshape(1, ws))


def _top_mask(s, n_avail, n_sel):
    lane = lax.broadcasted_iota(jnp.int32, s.shape, 1)
    s = jnp.where(lane < n_avail, s, -jnp.inf)
    sel = jnp.zeros(s.shape, F32)
    for _ in range(n_sel):
        mx = jnp.max(s, axis=-1, keepdims=True)
        first = jnp.min(jnp.where(s == mx, lane, s.shape[1]), axis=-1, keepdims=True)
        hit = lane == first
        sel = jnp.where(hit, 1.0, sel)
        s = jnp.where(hit, -jnp.inf, s)
    return jnp.where(lane < n_avail, sel, 0.0)


def _moba_prompt_body(q_ref, k_ref, v_ref, o_ref, kb_ref, vb_ref, km_ref, *, n_blk, n_sel):
    i = pl.program_id(1)
    blk = MOBA_BLOCK

    @pl.when(i == 0)
    def _():
        kb_ref[...] = k_ref[...].astype(BF16)
        vb_ref[...] = v_ref[...].astype(BF16)
        km_ref[...] = jnp.zeros(km_ref.shape, F32)
        for b in range(n_blk):
            km_ref[b:b + 1, :] = jnp.mean(k_ref[b * blk:(b + 1) * blk, :], axis=0, keepdims=True)

    q = q_ref[...]
    qs = (q * (HEAD_DIM ** -0.5)).astype(BF16)
    nt = (((1,), (1,)), ((), ()))

    row0 = pl.multiple_of(i * blk, blk)
    s = lax.dot_general(qs, kb_ref[pl.ds(row0, blk), :], nt, preferred_element_type=F32)
    r_id = lax.broadcasted_iota(jnp.int32, s.shape, 0)
    c_id = lax.broadcasted_iota(jnp.int32, s.shape, 1)
    s = jnp.where(c_id <= r_id, s, NEG)
    m = jnp.max(s, axis=-1, keepdims=True)
    p = jnp.exp(s - m)
    l = jnp.sum(p, axis=-1, keepdims=True)
    acc = jnp.dot(p.astype(BF16), vb_ref[pl.ds(row0, blk), :], preferred_element_type=F32)

    if n_sel > 0:
        sc = lax.dot_general(q, km_ref[...], nt, precision=lax.Precision.HIGHEST,
                             preferred_element_type=F32)
        sel = _top_mask(sc, i, n_sel)
        lane = lax.broadcasted_iota(jnp.int32, sel.shape, 1)

        def step(j, carry):
            m, l, acc = carry
            r0 = pl.multiple_of(j * blk, blk)
            on = jnp.max(jnp.where(lane == j, sel, 0.0), axis=-1, keepdims=True)
            s = lax.dot_general(qs, kb_ref[pl.ds(r0, blk), :], nt, preferred_element_type=F32)
            m_c = jnp.maximum(m, jnp.max(s, axis=-1, keepdims=True))
            p = jnp.exp(s - m_c)
            m_n = jnp.where(on > 0.0, m_c, m)
            alpha = jnp.exp(m - m_n)
            l = alpha * l + on * jnp.sum(p, axis=-1, keepdims=True)
            pv = jnp.dot(p.astype(BF16), vb_ref[pl.ds(r0, blk), :], preferred_element_type=F32)
            acc = alpha * acc + on * pv
            return m_n, l, acc

        m, l, acc = lax.fori_loop(0, i, step, (m, l, acc))

    o_ref[...] = acc / l


def moba_prompt(qkvu, n_heads):
    s_len = qkvu.shape[0]
    blk = MOBA_BLOCK
    n_blk = s_len // blk
    n_cand = (s_len - 1) // blk
    n_sel = min(MOBA_TOPK, n_cand)
    assert s_len % blk == 0 and n_blk <= LANES
    return pl.pallas_call(
        functools.partial(_moba_prompt_body, n_blk=n_blk, n_sel=n_sel),
        grid=(n_heads, n_blk),
        in_specs=[pl.BlockSpec((blk, HEAD_DIM), lambda h, i: (i, h)),
                  pl.BlockSpec((s_len, HEAD_DIM), lambda h, i: (0, n_heads + h)),
                  pl.BlockSpec((s_len, HEAD_DIM), lambda h, i: (0, 2 * n_heads + h))],
        out_specs=pl.BlockSpec((blk, HEAD_DIM), lambda h, i: (i, h)),
        out_shape=jax.ShapeDtypeStruct((s_len, n_heads * HEAD_DIM), F32),
        scratch_shapes=[pltpu.VMEM((s_len, HEAD_DIM), BF16),
                        pltpu.VMEM((s_len, HEAD_DIM), BF16),
                        pltpu.VMEM((LANES, HEAD_DIM), F32)],
        compiler_params=_cparams("arbitrary", "arbitrary"),
        name="moba_prompt",
    )(qkvu, qkvu, qkvu)


def _head_sums(x, n_heads):
    parts = []
    for h in range(n_heads):
        sm = jnp.sum(x[:, h * HEAD_DIM:(h + 1) * HEAD_DIM], axis=-1, keepdims=True)
        parts.append(jnp.broadcast_to(sm, (x.shape[0], HEAD_DIM)))
    return jnp.concatenate(parts, axis=1)


def _expand_stat(col, n_heads, t):
    parts = [jnp.broadcast_to(col[h * t:(h + 1) * t, :], (t, HEAD_DIM)) for h in range(n_heads)]
    return jnp.concatenate(parts, axis=1)


def _moba_sample_body(pt_ref, q_ref, kn_ref, vn_ref, ka_ref, kb_ref, va_ref, vb_ref, o_ref,
                      m_ref, l_ref, acc_ref, ks_ref, *, n_heads, n_blk, n_sel, t_new):
    del pt_ref
    b = pl.program_id(1)
    w = n_heads * HEAD_DIM
    pair_w = 2 * HEAD_DIM
    n_pair = n_heads // 2
    nt = (((1,), (1,)), ((), ()))

    q = q_ref[...]
    qs = q * (HEAD_DIM ** -0.5)
    lane_p = lax.broadcasted_iota(jnp.int32, (t_new, pair_w), 1)

    k_blk = jnp.concatenate([ka_ref[0], kb_ref[0]], axis=0)
    v_blk = jnp.concatenate([va_ref[0], vb_ref[0]], axis=0)
    ks_ref[b] = jnp.broadcast_to(jnp.sum(k_blk, axis=0, keepdims=True), (t_new, w))

    s_parts = []
    for c in range(n_pair):
        qc = qs[:, c * pair_w:(c + 1) * pair_w]
        qpair = jnp.concatenate([jnp.where(lane_p < HEAD_DIM, qc, 0.0),
                                 jnp.where(lane_p >= HEAD_DIM, qc, 0.0)], axis=0).astype(BF16)
        kc = k_blk[:, c * pair_w:(c + 1) * pair_w].astype(BF16)
        s_parts.append(lax.dot_general(qpair, kc, nt, preferred_element_type=F32))
    s = jnp.concatenate(s_parts, axis=0)
    m = jnp.max(s, axis=-1, keepdims=True)
    p = jnp.exp(s - m)
    l = jnp.sum(p, axis=-1, keepdims=True)
    pb = p.astype(BF16)
    o_parts = []
    for c in range(n_pair):
        vc = v_blk[:, c * pair_w:(c + 1) * pair_w].astype(BF16)
        r = jnp.dot(pb[c * 2 * t_new:(c + 1) * 2 * t_new, :], vc, preferred_element_type=F32)
        o_parts.append(r[:t_new, :HEAD_DIM])
        o_parts.append(r[t_new:, HEAD_DIM:])
    acc_ref[b] = jnp.concatenate(o_parts, axis=1)
    m_ref[b] = _expand_stat(m, n_heads, t_new)
    l_ref[b] = _expand_stat(l, n_heads, t_new)

    @pl.when(b == n_blk - 1)
    def _():
        sc = [_head_sums(q * (ks_ref[j] * (1.0 / MOBA_BLOCK)), n_heads) for j in range(n_blk)]
        sel = [jnp.zeros((t_new, w), F32) for _ in range(n_blk)]
        for _ in range(n_sel):
            mx = sc[0]
            for j in range(1, n_blk):
                mx = jnp.maximum(mx, sc[j])
            taken = jnp.zeros((t_new, w), F32)
            for j in range(n_blk):
                hit = jnp.logical_and(sc[j] == mx, taken == 0.0)
                sel[j] = jnp.where(hit, 1.0, sel[j])
                sc[j] = jnp.where(hit, -jnp.inf, sc[j])
                taken = jnp.where(hit, 1.0, taken)

        kn = kn_ref[...]
        vn = vn_ref[...]
        t_id = lax.broadcasted_iota(jnp.int32, (t_new, w), 0)
        s_own = []
        for tk in range(t_new):
            so = _head_sums(qs * kn[tk:tk + 1, :], n_heads)
            s_own.append(jnp.where(t_id >= tk, so, NEG))
        m_all = s_own[0]
        for tk in range(1, t_new):
            m_all = jnp.maximum(m_all, s_own[tk])
        for j in range(n_blk):
            m_all = jnp.maximum(m_all, jnp.where(sel[j] > 0.0, m_ref[j], NEG))
        num = jnp.zeros((t_new, w), F32)
        den = jnp.zeros((t_new, w), F32)
        for j in range(n_blk):
            wj = sel[j] * jnp.exp(jnp.where(sel[j] > 0.0, m_ref[j], NEG) - m_all)
            num = num + wj * acc_ref[j]
            den = den + wj * l_ref[j]
        for tk in range(t_new):
            pk = jnp.exp(s_own[tk] - m_all)
            num = num + pk * vn[tk:tk + 1, :]
            den = den + pk
        o_ref[...] = num / den


def moba_sample(qkvu, cache_k, cache_v, page_table, n_heads, t_new, past_len):
    n_seq = page_table.shape[0]
    w = n_heads * HEAD_DIM
    ppb = MOBA_BLOCK // PAGE_SIZE
    n_blk = past_len // MOBA_BLOCK
    n_sel = min(MOBA_TOPK, n_blk)
    assert past_len % MOBA_BLOCK == 0 and n_blk >= 1 and ppb == 2 and t_new <= MOBA_BLOCK
    n_pages = page_table.shape[1]
    pt = page_table.reshape(-1)

    def page_spec(off):
        return pl.BlockSpec((1, PAGE_SIZE, w), lambda s, b, pt: (pt[s * n_pages + b * ppb + off], 0, 0))

    return pl.pallas_call(
        functools.partial(_moba_sample_body, n_heads=n_heads, n_blk=n_blk, n_sel=n_sel, t_new=t_new),
        grid_spec=pltpu.PrefetchScalarGridSpec(
            num_scalar_prefetch=1,
            grid=(n_seq, n_blk),
            in_specs=[pl.BlockSpec((t_new, w), lambda s, b, pt: (s, 0)),
                      pl.BlockSpec((t_new, w), lambda s, b, pt: (s, 1)),
                      pl.BlockSpec((t_new, w), lambda s, b, pt: (s, 2)),
                      page_spec(0), page_spec(1), page_spec(0), page_spec(1)],
            out_specs=pl.BlockSpec((t_new, w), lambda s, b, pt: (s, 0)),
            scratch_shapes=[pltpu.VMEM((n_blk, t_new, w), F32),
                            pltpu.VMEM((n_blk, t_new, w), F32),
                            pltpu.VMEM((n_blk, t_new, w), F32),
                            pltpu.VMEM((n_blk, t_new, w), F32)]),
        out_shape=jax.ShapeDtypeStruct((n_seq * t_new, w), F32),
        compiler_params=_cparams("arbitrary", "arbitrary"),
        name="moba_sample",
    )(pt, qkvu, qkvu, qkvu, cache_k, cache_k, cache_v, cache_v)


def _cpow(lr, li, step, e):
    mag = jnp.exp(lr * step * e)
    ang = li * step * e
    return mag * jnp.cos(ang), mag * jnp.sin(ang)


def _zoh_gain(lr, li, step):
    a_re, a_im = _cpow(lr, li, step, 1.0)
    den = lr * lr + li * li
    f_re = ((a_re - 1.0) * lr + a_im * li) / den
    f_im = (a_im * lr - (a_re - 1.0) * li) / den
    return f_re, f_im


def _s5_coef_body(lrr_ref, lir_ref, lsr_ref, lrc_ref, lic_ref, lsc_ref,
                  btr_ref, bti_ref, ctr_ref, cti_ref,
                  w_ref, m_ref, v_ref, al_ref):
    gc, ch = S5_GROUP, S5_CHUNK
    gw = gc * ch
    pw = 2 * gw
    lr, li = lrr_ref[0], lir_ref[0]
    step = jnp.exp(lsr_ref[0])
    f_re, f_im = _zoh_gain(lr, li, step)
    bt_re, bt_im = btr_ref[0], bti_ref[0]
    bb_re = f_re * bt_re - f_im * bt_im
    bb_im = f_re * bt_im + f_im * bt_re
    row = lax.broadcasted_iota(jnp.int32, (pw, gw), 0)
    lane = lax.broadcasted_iota(jnp.int32, (pw, gw), 1)
    same = (row // gw) == (lane // S5_STATE)
    e_w = (ch - 1 - (row % gw) // gc).astype(F32)
    p_re, p_im = _cpow(lr, li, step, e_w)
    w_re = jnp.where(same, p_re * bb_re - p_im * bb_im, 0.0)
    w_im = jnp.where(same, p_re * bb_im + p_im * bb_re, 0.0)
    w_ref[0] = jnp.concatenate([w_re, w_im], axis=1).astype(BF16)
    al_re, al_im = _cpow(lr, li, step, float(ch))
    al_ref[0] = jnp.concatenate([jnp.broadcast_to(al_re, (8, gw)),
                                 jnp.broadcast_to(al_im, (8, gw))], axis=1)

    lrc, lic = lrc_ref[0], lic_ref[0]
    stc = jnp.exp(lsc_ref[0])
    ct_re, ct_im = ctr_ref[0], cti_ref[0]
    row2 = lax.broadcasted_iota(jnp.int32, (gw, pw), 0)
    lane2 = lax.broadcasted_iota(jnp.int32, (gw, pw), 1)
    same2 = (row2 // S5_STATE) == (lane2 // gw)
    tau = ((lane2 % gw) // gc).astype(F32)
    q_re, q_im = _cpow(lrc, lic, stc, tau + 1.0)
    v_re = jnp.where(same2, q_re * ct_re - q_im * ct_im, 0.0)
    v_im = jnp.where(same2, q_re * ct_im + q_im * ct_re, 0.0)
    v_ref[0] = jnp.concatenate([v_re, -v_im], axis=0).astype(BF16)
    q_re, q_im = _cpow(lrc, lic, stc, tau)
    e_re = jnp.where(same2, q_re * ct_re - q_im * ct_im, 0.0)
    e_im = jnp.where(same2, q_re * ct_im + q_im * ct_re, 0.0)
    hp = lax.Precision.HIGHEST
    k_all = (jnp.dot(bb_re[:gc], e_re, precision=hp, preferred_element_type=F32)
             - jnp.dot(bb_im[:gc], e_im, precision=hp, preferred_element_type=F32))
    lane3 = lax.broadcasted_iota(jnp.int32, (gc, pw), 1)
    blocks = []
    for g in range(2):
        for j in range(ch):
            rolled = k_all if j == 0 else pltpu.roll(k_all, gc * j, axis=1)
            keep = jnp.logical_and((lane3 % gw) >= gc * j, (lane3 // gw) == g)
            blocks.append(jnp.where(keep, rolled, 0.0))
    m_ref[0] = jnp.concatenate(blocks, axis=0).astype(BF16)


def s5_coef(lam_re, lam_im, log_step, b_re, b_im, c_re, c_im):
    g, p = lam_re.shape
    gc, ch = S5_GROUP, S5_CHUNK
    assert p == S5_STATE and b_re.shape == (g, p, gc) and g % 2 == 0
    np_ = g // 2
    gw, pw = gc * ch, 2 * gc * ch
    row = lambda a: a.reshape(np_, 1, 2 * p)
    col = lambda a: a.reshape(np_, 2 * p, 1)
    ls = jnp.broadcast_to(log_step[:, None], (g, p))

    def bt(b):
        x = jnp.swapaxes(b, 1, 2).reshape(np_, 2, gc, p).transpose(0, 2, 1, 3).reshape(np_, 1, gc, 2 * p)
        return jnp.broadcast_to(x, (np_, 2 * ch, gc, 2 * p)).reshape(np_, pw, 2 * p)

    def ct(c):
        x = jnp.swapaxes(c, 1, 2).reshape(np_, 2 * p, 1, gc)
        return jnp.broadcast_to(x, (np_, 2 * p, 2 * ch, gc)).reshape(np_, 2 * p, pw)

    vec_r = pl.BlockSpec((1, 1, 2 * p), lambda i: (i, 0, 0))
    vec_c = pl.BlockSpec((1, 2 * p, 1), lambda i: (i, 0, 0))
    mat = pl.BlockSpec((1, pw, pw), lambda i: (i, 0, 0))
    return pl.pallas_call(
        _s5_coef_body,
        grid=(np_,),
        in_specs=[vec_r, vec_r, vec_r, vec_c, vec_c, vec_c,
                  pl.BlockSpec((1, pw, 2 * p), lambda i: (i, 0, 0)),
                  pl.BlockSpec((1, pw, 2 * p), lambda i: (i, 0, 0)),
                  pl.BlockSpec((1, 2 * p, pw), lambda i: (i, 0, 0)),
                  pl.BlockSpec((1, 2 * p, pw), lambda i: (i, 0, 0))],
        out_specs=[mat, mat, mat, pl.BlockSpec((1, 8, pw), lambda i: (i, 0, 0))],
        out_shape=[jax.ShapeDtypeStruct((np_, pw, pw), BF16),
                   jax.ShapeDtypeStruct((np_, pw, pw), BF16),
                   jax.ShapeDtypeStruct((np_, pw, pw), BF16),
                   jax.ShapeDtypeStruct((np_, 8, pw), F32)],
        compiler_params=_cparams("arbitrary"),
        name="s5_coef",
    )(row(lam_re), row(lam_im), row(ls), col(lam_re), col(lam_im), col(ls),
      bt(b_re), bt(b_im), ct(c_re), ct(c_im))


def _s5_x_body(u_ref, w_ref, xr_ref, xi_ref):
    x = jnp.dot(u_ref[0].astype(BF16), w_ref[0], preferred_element_type=F32)
    hw = x.shape[1] // 2
    xr_ref[...] = x[:, :hw]
    xi_ref[...] = x[:, hw:]


def s5_x(u_pair, w_pair):
    np_, n, pw = u_pair.shape
    hw = pw // 2
    out = pl.BlockSpec((n, hw), lambda i: (0, i))
    return pl.pallas_call(
        _s5_x_body,
        grid=(np_,),
        in_specs=[pl.BlockSpec((1, n, pw), lambda i: (i, 0, 0)),
                  pl.BlockSpec((1, pw, pw), lambda i: (i, 0, 0))],
        out_specs=[out, out],
        out_shape=[jax.ShapeDtypeStruct((n, np_ * hw), F32)] * 2,
        compiler_params=_cparams("arbitrary"),
        name="s5_x",
    )(u_pair, w_pair)


def _s5_scan_body(xr_ref, xi_ref, ar_ref, ai_ref, hr_ref, hi_ref, fr_ref, fi_ref):
    n, wd = xr_ref.shape
    a_re = ar_ref[0:1, :]
    a_im = ai_ref[0:1, :]

    def tile(k, carry):
        h_re, h_im = carry
        r0 = pl.multiple_of(k * 8, 8)
        x_re = xr_ref[pl.ds(r0, 8), :]
        x_im = xi_ref[pl.ds(r0, 8), :]
        rows_re, rows_im = [], []
        for s in range(8):
            rows_re.append(h_re)
            rows_im.append(h_im)
            n_re = a_re * h_re - a_im * h_im + x_re[s:s + 1, :]
            n_im = a_re * h_im + a_im * h_re + x_im[s:s + 1, :]
            h_re, h_im = n_re, n_im
        hr_ref[pl.ds(r0, 8), :] = jnp.concatenate(rows_re, axis=0)
        hi_ref[pl.ds(r0, 8), :] = jnp.concatenate(rows_im, axis=0)
        return h_re, h_im

    z = jnp.zeros((1, wd), F32)
    h_re, h_im = lax.fori_loop(0, n // 8, tile, (z, z))
    fr_ref[...] = jnp.broadcast_to(h_re, fr_ref.shape)
    fi_ref[...] = jnp.broadcast_to(h_im, fi_ref.shape)


def s5_scan(x_re, x_im, al_re, al_im, tw=1024):
    n, wd = x_re.shape
    assert n % 8 == 0
    blk = pl.BlockSpec((n, tw), lambda j: (0, j))
    vec = pl.BlockSpec((8, tw), lambda j: (0, j))
    return pl.pallas_call(
        _s5_scan_body,
        grid=(wd // tw,),
        in_specs=[blk, blk, vec, vec],
        out_specs=[blk, blk, vec, vec],
        out_shape=[jax.ShapeDtypeStruct((n, wd), F32)] * 2 + [jax.ShapeDtypeStruct((8, wd), F32)] * 2,
        compiler_params=_cparams("arbitrary"),
        name="s5_scan",
    )(x_re, x_im, al_re, al_im)


def _s5_y_body(u_ref, hr_ref, hi_ref, m_ref, v_ref, y_ref):
    u = u_ref[0].astype(BF16)
    h = jnp.concatenate([hr_ref[...], hi_ref[...]], axis=1).astype(BF16)
    y_ref[0] = (jnp.dot(u, m_ref[0], preferred_element_type=F32)
                + jnp.dot(h, v_ref[0], preferred_element_type=F32))


def s5_y(u_pair, hin_re, hin_im, m_pair, v_pair):
    np_, n, pw = u_pair.shape
    hw = pw // 2
    st = pl.BlockSpec((n, hw), lambda i: (0, i))
    mat = pl.BlockSpec((1, pw, pw), lambda i: (i, 0, 0))
    return pl.pallas_call(
        _s5_y_body,
        grid=(np_,),
        in_specs=[pl.BlockSpec((1, n, pw), lambda i: (i, 0, 0)), st, st, mat, mat],
        out_specs=pl.BlockSpec((1, n, pw), lambda i: (i, 0, 0)),
        out_shape=jax.ShapeDtypeStruct((np_, n, pw), F32),
        compiler_params=_cparams("arbitrary"),
        name="s5_y",
    )(u_pair, hin_re, hin_im, m_pair, v_pair)


def _s5_step_body(hr_ref, hi_ref, xr_ref, xi_ref, ar_ref, ai_ref, or_ref, oi_ref):
    a_re = ar_ref[0:1, :]
    a_im = ai_ref[0:1, :]
    h_re, h_im = hr_ref[...], hi_ref[...]
    or_ref[...] = a_re * h_re - a_im * h_im + xr_ref[...]
    oi_ref[...] = a_re * h_im + a_im * h_re + xi_ref[...]


def s5_step(h_re, h_im, x_re, x_im, al_re, al_im, tw=2048):
    n, wd = h_re.shape
    blk = pl.BlockSpec((n, tw), lambda j: (0, j))
    vec = pl.BlockSpec((8, tw), lambda j: (0, j))
    return pl.pallas_call(
        _s5_step_body,
        grid=(wd // tw,),
        in_specs=[blk, blk, blk, blk, vec, vec],
        out_specs=[blk, blk],
        out_shape=[jax.ShapeDtypeStruct((n, wd), F32)] * 2,
        compiler_params=_cparams("arbitrary"),
        name="s5_step",
    )(h_re, h_im, x_re, x_im, al_re, al_im)


def _s5_glu_body(y_ref, u_ref, d_ref, w_ref, b_ref, o_ref, z_ref, *, tn):
    j = pl.program_id(1)

    @pl.when(j == 0)
    def _():
        z_ref[...] = jax.nn.gelu(y_ref[...] + d_ref[...] * u_ref[...])

    col = pl.multiple_of(j * tn, tn)
    lin = jnp.dot(z_ref[...].astype(BF16), w_ref[...], preferred_element_type=F32) + b_ref[...]
    o_ref[...] = z_ref[:, pl.ds(col, tn)] * jax.nn.sigmoid(lin)


def s5_glu(y, qkvu, d_skip, w_glu, b_glu, u_col_block, tm=512, tn=512):
    t, ws = y.shape
    tm = min(tm, t)
    return pl.pallas_call(
        functools.partial(_s5_glu_body, tn=tn),
        grid=(t // tm, ws // tn),
        in_specs=[pl.BlockSpec((tm, ws), lambda i, j: (i, 0)),
                  pl.BlockSpec((tm, ws), lambda i, j: (i, u_col_block)),
                  pl.BlockSpec((1, ws), lambda i, j: (0, 0)),
                  pl.BlockSpec((ws, tn), lambda i, j: (0, j)),
                  pl.BlockSpec((1, tn), lambda i, j: (0, j))],
        out_specs=pl.BlockSpec((tm, tn), lambda i, j: (i, j)),
        out_shape=jax.ShapeDtypeStruct((t, ws), F32),
        scratch_shapes=[pltpu.VMEM((tm, ws), F32)],
        compiler_params=_cparams("arbitrary", "arbitrary"),
        name="s5_glu",
    )(y, qkvu, d_skip.reshape(1, ws), w_glu, b_glu.reshape(1, ws))


def _to_pairs(u, n_groups):
    n = u.shape[0] // S5_CHUNK
    x = u.reshape(n, S5_CHUNK, n_groups // 2, 2, S5_GROUP)
    return x.transpose(2, 0, 3, 1, 4).reshape(n_groups // 2, n, 2 * S5_CHUNK * S5_GROUP)


def _from_pairs(y, n_groups):
    np_, n, _ = y.shape
    x = y.reshape(np_, n, 2, S5_CHUNK, S5_GROUP)
    return x.transpose(1, 3, 0, 2, 4).reshape(n * S5_CHUNK, n_groups * S5_GROUP)


def s5_mix(qkvu, coef, d_skip, w_glu, b_glu, ssm_width, h0, seq_len):
    w_pair, m_pair, v_pair, al = coef
    n_groups = ssm_width // S5_GROUP
    t = qkvu.shape[0]
    u = qkvu[:, qkvu.shape[1] - ssm_width:]
    u_pair = _to_pairs(u, n_groups)
    gw = S5_GROUP * S5_CHUNK
    al_re = al[:, :, :gw].transpose(1, 0, 2).reshape(8, -1)
    al_im = al[:, :, gw:].transpose(1, 0, 2).reshape(8, -1)
    x_re, x_im = s5_x(u_pair, w_pair)
    if h0 is None:
        hin_re, hin_im, f_re, f_im = s5_scan(x_re, x_im, al_re, al_im)
        f_re, f_im = f_re[:1], f_im[:1]
    else:
        assert seq_len == S5_CHUNK
        hin_re, hin_im = h0
        f_re, f_im = s5_step(hin_re, hin_im, x_re, x_im, al_re, al_im)
    y = _from_pairs(s5_y(u_pair, hin_re, hin_im, m_pair, v_pair), n_groups)
    o_ssm = s5_glu(y, qkvu, d_skip, w_glu, b_glu, qkvu.shape[1] // ssm_width - 1)
    return o_ssm, f_re, f_im


def _layer(x, mod, attend, h0, seq_len, lw, coef, dims):
    d_model, attn_w, ssm_w = dims
    h = norm_mod(x, lw["g_norm1"], mod, 1, 0)
    qkvu = inproj(h, lw["w_in"], lw["g_q"], lw["g_k"], attn_w)
    o_attn = attend(qkvu)
    o_ssm, f_re, f_im = s5_mix(qkvu, coef, lw["s5_d"], lw["w_glu"], lw["b_glu"], ssm_w, h0, seq_len)
    mix = rms2(o_attn, o_ssm, lw["g_attn_out"], lw["g_ssm_out"])
    x1 = matmul_res(mix, lw["w_out"], x, mod, 2, tm=1024, tn=512)
    h2 = norm_mod(x1, lw["g_norm2"], mod, 4, 3)
    hid = gateup(h2, lw["w_gate"], lw["w_up"])
    y = matmul_res(hid, lw["w_down"], x1, mod, 5, tm=512, tn=256)
    k = qkvu[:, attn_w:2 * attn_w]
    v = qkvu[:, 2 * attn_w:3 * attn_w]
    return y, k, v, f_re, f_im


def kernel(x_prompt, x_sample, cache_k, cache_v, state_s5_re, state_s5_im, page_table, c_prompt, c_sample, w_ada, b_ada, g_norm1, w_in, g_q, g_k, s5_lam_re, s5_lam_im, s5_log_step, s5_b_re, s5_b_im, s5_c_re, s5_c_im, s5_d, w_glu, b_glu, g_attn_out, g_ssm_out, w_out, g_norm2, w_gate, w_up, w_down):
    depth = w_ada.shape[0]
    batch, seq, d_model = x_prompt.shape
    dec_batch, dec_seq, _ = x_sample.shape
    n_heads, head_dim = cache_k.shape[3], cache_k.shape[4]
    assert head_dim == HEAD_DIM and cache_k.shape[2] == PAGE_SIZE and batch == 1
    attn_w = n_heads * head_dim
    ssm_w = d_model - attn_w
    n_groups, n_state = state_s5_re.shape[2], state_s5_re.shape[3]
    past_len = page_table.shape[1] * PAGE_SIZE
    dims = (d_model, attn_w, ssm_w)

    y_p = x_prompt.reshape(batch * seq, d_model)
    y_s = x_sample.reshape(dec_batch * dec_seq, d_model)
    outs = [[] for _ in range(8)]
    for l in range(depth):
        lw = dict(g_norm1=g_norm1[l], w_in=w_in[l].astype(BF16), g_q=g_q[l], g_k=g_k[l],
                  s5_d=s5_d[l], w_glu=w_glu[l].astype(BF16), b_glu=b_glu[l],
                  g_attn_out=g_attn_out[l], g_ssm_out=g_ssm_out[l], w_out=w_out[l].astype(BF16),
                  g_norm2=g_norm2[l], w_gate=w_gate[l].astype(BF16), w_up=w_up[l].astype(BF16),
                  w_down=w_down[l].astype(BF16))
        coef = s5_coef(s5_lam_re[l], s5_lam_im[l], s5_log_step[l], s5_b_re[l], s5_b_im[l],
                       s5_c_re[l], s5_c_im[l])
        pad = (-(dec_batch + batch)) % 8
        c_all = jnp.concatenate([c_sample, c_prompt, jnp.zeros((pad, d_model), F32)], axis=0)
        mod = adaln(c_all, w_ada[l], b_ada[l])
        mod_p = mod[dec_batch:dec_batch + 1]
        mod_s = jnp.repeat(mod[:dec_batch], dec_seq, axis=0)
        ck = cache_k[l].reshape(cache_k.shape[1], PAGE_SIZE, attn_w)
        cv = cache_v[l].reshape(cache_v.shape[1], PAGE_SIZE, attn_w)
        h0 = (state_s5_re[l].reshape(dec_batch, n_groups * n_state),
              state_s5_im[l].reshape(dec_batch, n_groups * n_state))

        y_p, kp, vp, hrp, hip = _layer(
            y_p, mod_p, functools.partial(moba_prompt, n_heads=n_heads), None, seq, lw, coef, dims)
        y_s, ks, vs, hrs, his = _layer(
            y_s, mod_s,
            functools.partial(moba_sample, cache_k=ck, cache_v=cv, page_table=page_table,
                              n_heads=n_heads, t_new=dec_seq, past_len=past_len),
            h0, dec_seq, lw, coef, dims)
        vals = (kp.reshape(batch, seq, n_heads, head_dim), vp.reshape(batch, seq, n_heads, head_dim),
                hrp.reshape(batch, n_groups, n_state), hip.reshape(batch, n_groups, n_state),
                ks.reshape(dec_batch, dec_seq, n_heads, head_dim),
                vs.reshape(dec_batch, dec_seq, n_heads, head_dim),
                hrs.reshape(dec_batch, n_groups, n_state), his.reshape(dec_batch, n_groups, n_state))
        for o, val in zip(outs, vals):
            o.append(val)
    return (y_p.reshape(batch, seq, d_model), y_s.reshape(dec_batch, dec_seq, d_model),
            *[jnp.stack(o) for o in outs])
```

```python
import functools

import jax
import jax.numpy as jnp
from jax import lax
from jax.experimental import pallas as pl
from jax.experimental.pallas import tpu as pltpu

F32 = jnp.float32
BF16 = jnp.bfloat16

HEAD_DIM = 128
MOBA_BLOCK = 256
MOBA_TOPK = 3
PAST_UNROLL = 4
PAGE_SIZE = 128
S5_GROUP = 16
S5_STATE = 64
S5_CHUNK = 8
EPS = 1e-6
NEG = -1e30

LANES = 128
SUBLANES = 8
S5_LB = LANES // S5_GROUP
VMEM_LIMIT = 56 * 1024 * 1024


def _cparams(*sem):
    return pltpu.CompilerParams(dimension_semantics=sem, vmem_limit_bytes=VMEM_LIMIT)


def _adaln_body(c_ref, w_ref, b_ref, o_ref):
    c = c_ref[...]
    s = (c * jax.nn.sigmoid(c)).astype(BF16)
    o_ref[...] = jnp.dot(s, w_ref[...].astype(BF16), preferred_element_type=F32) + b_ref[...]


def adaln(c, w_ada, b_ada, tn=512):
    m, d = c.shape
    n = w_ada.shape[1]
    return pl.pallas_call(
        _adaln_body,
        grid=(n // tn,),
        in_specs=[pl.BlockSpec((m, d), lambda j: (0, 0)),
                  pl.BlockSpec((d, tn), lambda j: (0, j)),
                  pl.BlockSpec((1, tn), lambda j: (0, j))],
        out_specs=pl.BlockSpec((m, tn), lambda j: (0, j)),
        out_shape=jax.ShapeDtypeStruct((m, n), F32),
        compiler_params=_cparams("arbitrary"),
        name="adaln",
    )(c, w_ada, b_ada.reshape(1, n))


def _mod_spec(mod, tm, tn, chunk, n_col_blocks):
    if mod.shape[0] == 1:
        return pl.BlockSpec((1, tn), lambda i, j: (0, chunk * n_col_blocks + j))
    return pl.BlockSpec((tm, tn), lambda i, j: (i, chunk * n_col_blocks + j))


def _norm_mod_body(x_ref, g_ref, sc_ref, sh_ref, o_ref):
    x = x_ref[...]
    r = lax.rsqrt(jnp.mean(x * x, axis=-1, keepdims=True) + EPS)
    h = ((x * r) * g_ref[...]) * (1.0 + sc_ref[...]) + sh_ref[...]
    o_ref[...] = h.astype(BF16)


def norm_mod(x, g, mod, sc_chunk, sh_chunk, tm=256):
    t, d = x.shape
    tm = min(tm, t)
    return pl.pallas_call(
        _norm_mod_body,
        grid=(t // tm, 1),
        in_specs=[pl.BlockSpec((tm, d), lambda i, j: (i, 0)),
                  pl.BlockSpec((1, d), lambda i, j: (0, 0)),
                  _mod_spec(mod, tm, d, sc_chunk, 1),
                  _mod_spec(mod, tm, d, sh_chunk, 1)],
        out_specs=pl.BlockSpec((tm, d), lambda i, j: (i, 0)),
        out_shape=jax.ShapeDtypeStruct((t, d), BF16),
        compiler_params=_cparams("arbitrary", "arbitrary"),
        name="norm_mod",
    )(x, g.reshape(1, d), mod, mod)


def _inproj_body(h_ref, w_ref, gq_ref, gk_ref, o_ref, *, tn, attn_width):
    j = pl.program_id(1)
    acc = jnp.dot(h_ref[...], w_ref[...], preferred_element_type=F32)
    col0 = j * tn

    @pl.when(col0 < 2 * attn_width)
    def _():
        g = jnp.where(col0 < attn_width, gq_ref[...], gk_ref[...])
        for s in range(tn // HEAD_DIM):
            a = acc[:, s * HEAD_DIM:(s + 1) * HEAD_DIM]
            r = lax.rsqrt(jnp.mean(a * a, axis=-1, keepdims=True) + EPS)
            o_ref[:, s * HEAD_DIM:(s + 1) * HEAD_DIM] = (a * r) * g

    @pl.when(col0 >= 2 * attn_width)
    def _():
        o_ref[...] = acc


def inproj(h, w_in, g_q, g_k, attn_width, tm=1024, tn=512):
    t, d = h.shape
    n = w_in.shape[1]
    tm = min(tm, t)
    assert attn_width % tn == 0
    return pl.pallas_call(
        functools.partial(_inproj_body, tn=tn, attn_width=attn_width),
        grid=(t // tm, n // tn),
        in_specs=[pl.BlockSpec((tm, d), lambda i, j: (i, 0)),
                  pl.BlockSpec((d, tn), lambda i, j: (0, j)),
                  pl.BlockSpec((1, HEAD_DIM), lambda i, j: (0, 0)),
                  pl.BlockSpec((1, HEAD_DIM), lambda i, j: (0, 0))],
        out_specs=pl.BlockSpec((tm, tn), lambda i, j: (i, j)),
        out_shape=jax.ShapeDtypeStruct((t, n), F32),
        compiler_params=_cparams("arbitrary", "arbitrary"),
        name="inproj",
    )(h, w_in, g_q.reshape(1, HEAD_DIM), g_k.reshape(1, HEAD_DIM))


def _matmul_res_body(a_ref, w_ref, res_ref, gt_ref, o_ref):
    acc = jnp.dot(a_ref[...], w_ref[...], preferred_element_type=F32)
    o_ref[...] = res_ref[...] + gt_ref[...] * acc


def matmul_res(a, w, res, mod, gt_chunk, tm, tn):
    t, k = a.shape
    n = w.shape[1]
    tm = min(tm, t)
    return pl.pallas_call(
        _matmul_res_body,
        grid=(t // tm, n // tn),
        in_specs=[pl.BlockSpec((tm, k), lambda i, j: (i, 0)),
                  pl.BlockSpec((k, tn), lambda i, j: (0, j)),
                  pl.BlockSpec((tm, tn), lambda i, j: (i, j)),
                  _mod_spec(mod, tm, tn, gt_chunk, n // tn)],
        out_specs=pl.BlockSpec((tm, tn), lambda i, j: (i, j)),
        out_shape=jax.ShapeDtypeStruct((t, n), F32),
        compiler_params=_cparams("arbitrary", "arbitrary"),
        name="matmul_res",
    )(a, w, res, mod)


def _gateup_body(h_ref, wg_ref, wu_ref, o_ref):
    h = h_ref[...]
    g = jnp.dot(h, wg_ref[...], preferred_element_type=F32)
    u = jnp.dot(h, wu_ref[...], preferred_element_type=F32)
    o_ref[...] = ((g * jax.nn.sigmoid(g)) * u).astype(BF16)


def gateup(h, w_gate, w_up, tm=1024, tn=256):
    t, d = h.shape
    n = w_gate.shape[1]
    tm = min(tm, t)
    return pl.pallas_call(
        _gateup_body,
        grid=(t // tm, n // tn),
        in_specs=[pl.BlockSpec((tm, d), lambda i, j: (i, 0)),
                  pl.BlockSpec((d, tn), lambda i, j: (0, j)),
                  pl.BlockSpec((d, tn), lambda i, j: (0, j))],
        out_specs=pl.BlockSpec((tm, tn), lambda i, j: (i, j)),
        out_shape=jax.ShapeDtypeStruct((t, n), BF16),
        compiler_params=_cparams("arbitrary", "arbitrary"),
        name="gateup",
    )(h, w_gate, w_up)


def _rms2_body(a_ref, s_ref, ga_ref, gs_ref, o_ref, *, wa):
    a = a_ref[...]
    ra = lax.rsqrt(jnp.mean(a * a, axis=-1, keepdims=True) + EPS)
    o_ref[:, :wa] = ((a * ra) * ga_ref[...]).astype(BF16)
    s = s_ref[...]
    rs = lax.rsqrt(jnp.mean(s * s, axis=-1, keepdims=True) + EPS)
    o_ref[:, wa:] = ((s * rs) * gs_ref[...]).astype(BF16)


def rms2(o_attn, o_ssm, g_a, g_s, tm=256):
    t, wa = o_attn.shape
    ws = o_ssm.shape[1]
    tm = min(tm, t)
    return pl.pallas_call(
        functools.partial(_rms2_body, wa=wa),
        grid=(t // tm,),
        in_specs=[pl.BlockSpec((tm, wa), lambda i: (i, 0)),
                  pl.BlockSpec((tm, ws), lambda i: (i, 0)),
                  pl.BlockSpec((1, wa), lambda i: (0, 0)),
                  pl.BlockSpec((1, ws), lambda i: (0, 0))],
        out_specs=pl.BlockSpec((tm, wa + ws), lambda i: (i, 0)),
        out_shape=jax.ShapeDtypeStruct((t, wa + ws), BF16),
        compiler_params=_cparams("arbitrary"),
        name="rms2",
    )(o_attn, o_ssm, g_a.reshape(1, wa), g_s.reshape(1, ws))


def _top_mask_rows(s, n_avail, n_sel):
    row = lax.broadcasted_iota(jnp.int32, s.shape, 0)
    s = jnp.where(row < n_avail, s, -jnp.inf)
    sel = jnp.zeros(s.shape, F32)
    for _ in range(n_sel):
        mx = jnp.max(s, axis=0, keepdims=True)
        first = jnp.min(jnp.where(s == mx, row, s.shape[0]), axis=0, keepdims=True)
        hit = row == first
        sel = jnp.where(hit, 1.0, sel)
        s = jnp.where(hit, -jnp.inf, s)
    return jnp.where(row < n_avail, sel, 0.0)


def _moba_prompt_body(q_ref, k_ref, v_ref, o_ref, kb_ref, vt_ref, km_ref, sel_ref, *, n_blk, n_sel):
    i = pl.program_id(1)
    blk = MOBA_BLOCK

    @pl.when(i == 0)
    def _():
        kb_ref[...] = k_ref[...].astype(BF16)
        km_ref[...] = jnp.zeros(km_ref.shape, F32)
        for b in range(n_blk):
            rows = slice(b * blk, (b + 1) * blk)
            vt_ref[:, rows] = v_ref[rows, :].T.astype(BF16)
            km_ref[b:b + 1, :] = jnp.mean(k_ref[rows, :], axis=0, keepdims=True)

    q_t = q_ref[...].T
    qs_t = (q_t * (HEAD_DIM ** -0.5)).astype(BF16)

    row0 = pl.multiple_of(i * blk, blk)
    s = jnp.dot(kb_ref[pl.ds(row0, blk), :], qs_t, preferred_element_type=F32)
    k_id = lax.broadcasted_iota(jnp.int32, s.shape, 0)
    q_id = lax.broadcasted_iota(jnp.int32, s.shape, 1)
    s = jnp.where(k_id <= q_id, s, NEG)
    m = jnp.max(s, axis=0, keepdims=True)
    p = jnp.exp(s - m)
    l = jnp.sum(p, axis=0, keepdims=True)
    acc = jnp.dot(vt_ref[:, pl.ds(row0, blk)], p.astype(BF16), preferred_element_type=F32)

    if n_sel > 0:
        sc = jnp.dot(km_ref[...], q_t, precision=lax.Precision.HIGHEST, preferred_element_type=F32)
        sel_ref[...] = _top_mask_rows(sc, i, n_sel)

        def group(g, carry):
            m, l, acc = carry
            r0s, ons, ss = [], [], []
            for u in range(PAST_UNROLL):
                j = jnp.minimum(g * PAST_UNROLL + u, n_blk - 1)
                r0 = pl.multiple_of(j * blk, blk)
                r0s.append(r0)
                ons.append(sel_ref[pl.ds(j, 1), :])
                ss.append(jnp.dot(kb_ref[pl.ds(r0, blk), :], qs_t, preferred_element_type=F32))
            for r0, on, s in zip(r0s, ons, ss):
                m_c = jnp.maximum(m, jnp.max(s, axis=0, keepdims=True))
                p = jnp.exp(s - m_c)
                m_n = jnp.where(on > 0.0, m_c, m)
                alpha = jnp.exp(m - m_n)
                l = alpha * l + on * jnp.sum(p, axis=0, keepdims=True)
                pv = jnp.dot(vt_ref[:, pl.ds(r0, blk)], p.astype(BF16), preferred_element_type=F32)
                acc = alpha * acc + on * pv
                m = m_n
            return m, l, acc

        n_trips = (i + PAST_UNROLL - 1) // PAST_UNROLL
        m, l, acc = lax.fori_loop(0, n_trips, group, (m, l, acc))

    o_ref[...] = (acc / l).T


def moba_prompt(qkvu, n_heads):
    s_len = qkvu.shape[0]
    blk = MOBA_BLOCK
    n_blk = s_len // blk
    n_cand = (s_len - 1) // blk
    n_sel = min(MOBA_TOPK, n_cand)
    assert s_len % blk == 0 and n_blk <= LANES
    return pl.pallas_call(
        functools.partial(_moba_prompt_body, n_blk=n_blk, n_sel=n_sel),
        grid=(n_heads, n_blk),
        in_specs=[pl.BlockSpec((blk, HEAD_DIM), lambda h, i: (i, h)),
                  pl.BlockSpec((s_len, HEAD_DIM), lambda h, i: (0, n_heads + h)),
                  pl.BlockSpec((s_len, HEAD_DIM), lambda h, i: (0, 2 * n_heads + h))],
        out_specs=pl.BlockSpec((blk, HEAD_DIM), lambda h, i: (i, h)),
        out_shape=jax.ShapeDtypeStruct((s_len, n_heads * HEAD_DIM), F32),
        scratch_shapes=[pltpu.VMEM((s_len, HEAD_DIM), BF16),
                        pltpu.VMEM((HEAD_DIM, s_len), BF16),
                        pltpu.VMEM((LANES, HEAD_DIM), F32),
                        pltpu.VMEM((LANES, blk), F32)],
        compiler_params=_cparams("arbitrary", "arbitrary"),
        name="moba_prompt",
    )(qkvu, qkvu, qkvu)


def _head_sums(x, n_heads):
    parts = []
    for h in range(n_heads):
        sm = jnp.sum(x[:, h * HEAD_DIM:(h + 1) * HEAD_DIM], axis=-1, keepdims=True)
        parts.append(jnp.broadcast_to(sm, (x.shape[0], HEAD_DIM)))
    return jnp.concatenate(parts, axis=1)


def _expand_stat(col, n_heads, t):
    parts = [jnp.broadcast_to(col[h * t:(h + 1) * t, :], (t, HEAD_DIM)) for h in range(n_heads)]
    return jnp.concatenate(parts, axis=1)


def _moba_sample_body(pt_ref, q_ref, kn_ref, vn_ref, ka_ref, kb_ref, va_ref, vb_ref, o_ref,
                      m_ref, l_ref, acc_ref, ks_ref, *, n_heads, n_blk, n_sel, t_new):
    del pt_ref
    b = pl.program_id(1)
    w = n_heads * HEAD_DIM
    pair_w = 2 * HEAD_DIM
    n_pair = n_heads // 2
    nt = (((1,), (1,)), ((), ()))

    q = q_ref[...]
    qs = q * (HEAD_DIM ** -0.5)
    lane_p = lax.broadcasted_iota(jnp.int32, (t_new, pair_w), 1)

    k_blk = jnp.concatenate([ka_ref[...], kb_ref[...]], axis=0)
    v_blk = jnp.concatenate([va_ref[...], vb_ref[...]], axis=0)
    ks_ref[b] = jnp.broadcast_to(jnp.sum(k_blk, axis=0, keepdims=True), (t_new, w))

    s_parts = []
    for c in range(n_pair):
        qc = qs[:, c * pair_w:(c + 1) * pair_w]
        qpair = jnp.concatenate([jnp.where(lane_p < HEAD_DIM, qc, 0.0),
                                 jnp.where(lane_p >= HEAD_DIM, qc, 0.0)], axis=0).astype(BF16)
        kc = k_blk[:, c * pair_w:(c + 1) * pair_w].astype(BF16)
        s_parts.append(lax.dot_general(qpair, kc, nt, preferred_element_type=F32))
    s = jnp.concatenate(s_parts, axis=0)
    m = jnp.max(s, axis=-1, keepdims=True)
    p = jnp.exp(s - m)
    l = jnp.sum(p, axis=-1, keepdims=True)
    pb = p.astype(BF16)
    o_parts = []
    for c in range(n_pair):
        vc = v_blk[:, c * pair_w:(c + 1) * pair_w].astype(BF16)
        r = jnp.dot(pb[c * 2 * t_new:(c + 1) * 2 * t_new, :], vc, preferred_element_type=F32)
        o_parts.append(r[:t_new, :HEAD_DIM])
        o_parts.append(r[t_new:, HEAD_DIM:])
    acc_ref[b] = jnp.concatenate(o_parts, axis=1)
    m_ref[b] = _expand_stat(m, n_heads, t_new)
    l_ref[b] = _expand_stat(l, n_heads, t_new)

    @pl.when(b == n_blk - 1)
    def _():
        sc = [_head_sums(q * (ks_ref[j] * (1.0 / MOBA_BLOCK)), n_heads) for j in range(n_blk)]
        sel = [jnp.zeros((t_new, w), F32) for _ in range(n_blk)]
        for _ in range(n_sel):
            mx = sc[0]
            for j in range(1, n_blk):
                mx = jnp.maximum(mx, sc[j])
            taken = jnp.zeros((t_new, w), F32)
            for j in range(n_blk):
                hit = jnp.logical_and(sc[j] == mx, taken == 0.0)
                sel[j] = jnp.where(hit, 1.0, sel[j])
                sc[j] = jnp.where(hit, -jnp.inf, sc[j])
                taken = jnp.where(hit, 1.0, taken)

        kn = kn_ref[...]
        vn = vn_ref[...]
        t_id = lax.broadcasted_iota(jnp.int32, (t_new, w), 0)
        s_own = []
        for tk in range(t_new):
            so = _head_sums(qs * kn[tk:tk + 1, :], n_heads)
            s_own.append(jnp.where(t_id >= tk, so, NEG))
        m_all = s_own[0]
        for tk in range(1, t_new):
            m_all = jnp.maximum(m_all, s_own[tk])
        for j in range(n_blk):
            m_all = jnp.maximum(m_all, jnp.where(sel[j] > 0.0, m_ref[j], NEG))
        num = jnp.zeros((t_new, w), F32)
        den = jnp.zeros((t_new, w), F32)
        for j in range(n_blk):
            wj = sel[j] * jnp.exp(jnp.where(sel[j] > 0.0, m_ref[j], NEG) - m_all)
            num = num + wj * acc_ref[j]
            den = den + wj * l_ref[j]
        for tk in range(t_new):
            pk = jnp.exp(s_own[tk] - m_all)
            num = num + pk * vn[tk:tk + 1, :]
            den = den + pk
        o_ref[...] = num / den


def moba_sample(qkvu, cache_k, cache_v, layer, page_table, n_heads, t_new, past_len):
    n_seq = page_table.shape[0]
    w = n_heads * HEAD_DIM
    ppb = MOBA_BLOCK // PAGE_SIZE
    n_blk = past_len // MOBA_BLOCK
    n_sel = min(MOBA_TOPK, n_blk)
    assert past_len % MOBA_BLOCK == 0 and n_blk >= 1 and ppb == 2 and t_new <= MOBA_BLOCK
    n_pages = page_table.shape[1]
    pt = page_table.reshape(-1)

    def page_spec(off):
        return pl.BlockSpec((None, None, PAGE_SIZE, w),
                            lambda s, b, pt: (layer, pt[s * n_pages + b * ppb + off], 0, 0))

    return pl.pallas_call(
        functools.partial(_moba_sample_body, n_heads=n_heads, n_blk=n_blk, n_sel=n_sel, t_new=t_new),
        grid_spec=pltpu.PrefetchScalarGridSpec(
            num_scalar_prefetch=1,
            grid=(n_seq, n_blk),
            in_specs=[pl.BlockSpec((t_new, w), lambda s, b, pt: (s, 0)),
                      pl.BlockSpec((t_new, w), lambda s, b, pt: (s, 1)),
                      pl.BlockSpec((t_new, w), lambda s, b, pt: (s, 2)),
                      page_spec(0), page_spec(1), page_spec(0), page_spec(1)],
            out_specs=pl.BlockSpec((t_new, w), lambda s, b, pt: (s, 0)),
            scratch_shapes=[pltpu.VMEM((n_blk, t_new, w), F32),
                            pltpu.VMEM((n_blk, t_new, w), F32),
                            pltpu.VMEM((n_blk, t_new, w), F32),
                            pltpu.VMEM((n_blk, t_new, w), F32)]),
        out_shape=jax.ShapeDtypeStruct((n_seq * t_new, w), F32),
        compiler_params=_cparams("arbitrary", "arbitrary"),
        name="moba_sample",
    )(pt, qkvu, qkvu, qkvu, cache_k, cache_k, cache_v, cache_v)


def _cpowers(lr, li, step, n):
    mag = jnp.exp(lr * step)
    a_re, a_im = mag * jnp.cos(li * step), mag * jnp.sin(li * step)
    out = [(jnp.ones_like(a_re), jnp.zeros_like(a_im))]
    for _ in range(n):
        p_re, p_im = out[-1]
        out.append((p_re * a_re - p_im * a_im, p_re * a_im + p_im * a_re))
    return out


def _zoh_gain(a_re, a_im, lr, li):
    den = lr * lr + li * li
    f_re = ((a_re - 1.0) * lr + a_im * li) / den
    f_im = (a_im * lr - (a_re - 1.0) * li) / den
    return f_re, f_im


def _s5_coef_body(lrr_ref, lir_ref, lsr_ref, lrc_ref, lic_ref, lsc_ref,
                  btr_ref, bti_ref, ctr_ref, cti_ref, w_ref, mv_ref, al_ref):
    gc, ch, ns = S5_GROUP, S5_CHUNK, S5_STATE
    sw = S5_LB * ns
    aw = ch * LANES
    lr, li = lrr_ref[0], lir_ref[0]
    pw = _cpowers(lr, li, jnp.exp(lsr_ref[0]), ch)
    f_re, f_im = _zoh_gain(pw[1][0], pw[1][1], lr, li)
    bt_re, bt_im = btr_ref[0], bti_ref[0]
    row = lax.broadcasted_iota(jnp.int32, (LANES, sw), 0)
    lane = lax.broadcasted_iota(jnp.int32, (LANES, sw), 1)
    same = (row // gc) == (lane // ns)
    bb_re = jnp.where(same, f_re * bt_re - f_im * bt_im, 0.0)
    bb_im = jnp.where(same, f_re * bt_im + f_im * bt_re, 0.0)
    for j in range(ch):
        p_re, p_im = pw[ch - 1 - j]
        w_ref[0, j * LANES:(j + 1) * LANES, :] = jnp.concatenate(
            [p_re * bb_re - p_im * bb_im, p_re * bb_im + p_im * bb_re], axis=1).astype(BF16)
    al_re, al_im = pw[ch]
    al_ref[0] = jnp.concatenate([jnp.broadcast_to(al_re, (SUBLANES, sw)),
                                 jnp.broadcast_to(al_im, (SUBLANES, sw))], axis=1)

    lrc, lic = lrc_ref[0], lic_ref[0]
    qw = _cpowers(lrc, lic, jnp.exp(lsc_ref[0]), ch)
    row2 = lax.broadcasted_iota(jnp.int32, (sw, LANES), 0)
    lane2 = lax.broadcasted_iota(jnp.int32, (sw, LANES), 1)
    same2 = (row2 // ns) == (lane2 // gc)
    ct_re = jnp.where(same2, ctr_ref[0], 0.0)
    ct_im = jnp.where(same2, cti_ref[0], 0.0)
    hp = lax.Precision.HIGHEST
    mv_ref[0, :aw, :] = jnp.zeros((aw, aw), BF16)
    for e in range(ch + 1):
        q_re, q_im = qw[e]
        e_re = q_re * ct_re - q_im * ct_im
        e_im = q_re * ct_im + q_im * ct_re
        if e < ch:
            k_e = (jnp.dot(bb_re, e_re, precision=hp, preferred_element_type=F32)
                   - jnp.dot(bb_im, e_im, precision=hp, preferred_element_type=F32)).astype(BF16)
            for j in range(ch - e):
                t = j + e
                mv_ref[0, j * LANES:(j + 1) * LANES, t * LANES:(t + 1) * LANES] = k_e
        if e >= 1:
            t = e - 1
            mv_ref[0, aw:aw + sw, t * LANES:(t + 1) * LANES] = e_re.astype(BF16)
            mv_ref[0, aw + sw:, t * LANES:(t + 1) * LANES] = (-e_im).astype(BF16)


def s5_coef(lam_re, lam_im, log_step, b_re, b_im, c_re, c_im):
    g, p = lam_re.shape
    gc, ch = S5_GROUP, S5_CHUNK
    assert p == S5_STATE and b_re.shape == (g, p, gc) and g % S5_LB == 0
    nb = g // S5_LB
    sw, aw = S5_LB * p, ch * LANES
    row = lambda a: a.reshape(nb, 1, sw)
    col = lambda a: a.reshape(nb, sw, 1)
    ls = jnp.broadcast_to(log_step[:, None], (g, p))

    def bt(b):
        x = b.reshape(nb, S5_LB, p, gc).transpose(0, 3, 1, 2).reshape(nb, 1, gc, sw)
        return jnp.broadcast_to(x, (nb, S5_LB, gc, sw)).reshape(nb, LANES, sw)

    def ct(c):
        x = c.reshape(nb, S5_LB, gc, p).transpose(0, 1, 3, 2).reshape(nb, sw, 1, gc)
        return jnp.broadcast_to(x, (nb, sw, S5_LB, gc)).reshape(nb, sw, LANES)

    vec_r = pl.BlockSpec((1, 1, sw), lambda i: (i, 0, 0))
    vec_c = pl.BlockSpec((1, sw, 1), lambda i: (i, 0, 0))
    return pl.pallas_call(
        _s5_coef_body,
        grid=(nb,),
        in_specs=[vec_r, vec_r, vec_r, vec_c, vec_c, vec_c,
                  pl.BlockSpec((1, LANES, sw), lambda i: (i, 0, 0)),
                  pl.BlockSpec((1, LANES, sw), lambda i: (i, 0, 0)),
                  pl.BlockSpec((1, sw, LANES), lambda i: (i, 0, 0)),
                  pl.BlockSpec((1, sw, LANES), lambda i: (i, 0, 0))],
        out_specs=[pl.BlockSpec((1, aw, 2 * sw), lambda i: (i, 0, 0)),
                   pl.BlockSpec((1, aw + 2 * sw, aw), lambda i: (i, 0, 0)),
                   pl.BlockSpec((1, SUBLANES, 2 * sw), lambda i: (i, 0, 0))],
        out_shape=[jax.ShapeDtypeStruct((nb, aw, 2 * sw), BF16),
                   jax.ShapeDtypeStruct((nb, aw + 2 * sw, aw), BF16),
                   jax.ShapeDtypeStruct((nb, SUBLANES, 2 * sw), F32)],
        compiler_params=_cparams("arbitrary"),
        name="s5_coef",
    )(row(lam_re), row(lam_im), row(ls), col(lam_re), col(lam_im), col(ls),
      bt(b_re), bt(b_im), ct(c_re), ct(c_im))


def _s5_steps(u_ref, chunk0, rows):
    return [u_ref[pl.ds(chunk0 * S5_CHUNK + j, rows, stride=S5_CHUNK), :] for j in range(S5_CHUNK)]


def _s5_emit(z_ref, y, u, d, chunk0, rows):
    for t in range(S5_CHUNK):
        z = jax.nn.gelu(y[:, t * LANES:(t + 1) * LANES] + d * u[t])
        z_ref[pl.ds(chunk0 * S5_CHUNK + t, rows, stride=S5_CHUNK), :] = z


def _s5_prompt_body(u_ref, d_ref, w_ref, mv_ref, al_ref, z_ref, fr_ref, fi_ref, x_ref, h_ref,
                    *, n_chunks, rows):
    sw = x_ref.shape[1] // 2
    a_re = al_ref[0, 0:1, :sw]
    a_im = al_ref[0, 0:1, sw:]

    def tile(k, carry):
        h_re, h_im = carry
        r0 = pl.multiple_of(k * SUBLANES, SUBLANES)
        x = x_ref[pl.ds(r0, SUBLANES), :]
        rows_re, rows_im = [], []
        for s in range(SUBLANES):
            rows_re.append(h_re)
            rows_im.append(h_im)
            n_re = a_re * h_re - a_im * h_im + x[s:s + 1, :sw]
            n_im = a_re * h_im + a_im * h_re + x[s:s + 1, sw:]
            h_re, h_im = n_re, n_im
        h_ref[pl.ds(r0, SUBLANES), :] = jnp.concatenate(
            [jnp.concatenate(rows_re, axis=0), jnp.concatenate(rows_im, axis=0)], axis=1)
        return h_re, h_im

    state = (jnp.zeros((1, sw), F32), jnp.zeros((1, sw), F32))
    for c0 in range(0, n_chunks, rows):
        u = _s5_steps(u_ref, c0, rows)
        a = jnp.concatenate(u, axis=1).astype(BF16)
        x_ref[...] = jnp.dot(a, w_ref[0], preferred_element_type=F32)
        state = lax.fori_loop(0, rows // SUBLANES, tile, state)
        lhs = jnp.concatenate([a, h_ref[...].astype(BF16)], axis=1)
        y = jnp.dot(lhs, mv_ref[0], preferred_element_type=F32)
        _s5_emit(z_ref, y, u, d_ref[...], c0, rows)
    fr_ref[...] = jnp.broadcast_to(state[0], fr_ref.shape)
    fi_ref[...] = jnp.broadcast_to(state[1], fi_ref.shape)


def _s5_sample_body(u_ref, d_ref, w_ref, mv_ref, al_ref, hr_ref, hi_ref, z_ref, fr_ref, fi_ref,
                    *, n_chunks):
    sw = hr_ref.shape[1]
    a_re = al_ref[0, 0:1, :sw]
    a_im = al_ref[0, 0:1, sw:]
    u = _s5_steps(u_ref, 0, n_chunks)
    a = jnp.concatenate(u, axis=1).astype(BF16)
    x = jnp.dot(a, w_ref[0], preferred_element_type=F32)
    h_re, h_im = hr_ref[...], hi_ref[...]
    lhs = jnp.concatenate([a, h_re.astype(BF16), h_im.astype(BF16)], axis=1)
    y = jnp.dot(lhs, mv_ref[0], preferred_element_type=F32)
    _s5_emit(z_ref, y, u, d_ref[...], 0, n_chunks)
    fr_ref[...] = a_re * h_re - a_im * h_im + x[:, :sw]
    fi_ref[...] = a_re * h_im + a_im * h_re + x[:, sw:]


def s5_core(qkvu, coef, d_skip, ssm_width, h0, rows=256):
    w_all, mv_all, al = coef
    nb, aw, sw2 = w_all.shape
    sw = sw2 // 2
    t = qkvu.shape[0]
    n_chunks = t // S5_CHUNK
    assert t % S5_CHUNK == 0 and nb * LANES == ssm_width
    u_blk0 = (qkvu.shape[1] - ssm_width) // LANES
    in_specs = [pl.BlockSpec((t, LANES), lambda b: (0, u_blk0 + b)),
                pl.BlockSpec((1, LANES), lambda b: (0, b)),
                pl.BlockSpec((1, aw, sw2), lambda b: (b, 0, 0)),
                pl.BlockSpec((1, aw + sw2, aw), lambda b: (b, 0, 0)),
                pl.BlockSpec((1, SUBLANES, sw2), lambda b: (b, 0, 0))]
    z_spec = pl.BlockSpec((t, LANES), lambda b: (0, b))
    z_shape = jax.ShapeDtypeStruct((t, ssm_width), F32)
    args = (qkvu, d_skip.reshape(1, ssm_width), w_all, mv_all, al)
    if h0 is None:
        rows = min(rows, n_chunks)
        assert n_chunks % rows == 0 and rows % SUBLANES == 0
        st = pl.BlockSpec((SUBLANES, sw), lambda b: (0, b))
        z, f_re, f_im = pl.pallas_call(
            functools.partial(_s5_prompt_body, n_chunks=n_chunks, rows=rows),
            grid=(nb,),
            in_specs=in_specs,
            out_specs=[z_spec, st, st],
            out_shape=[z_shape] + [jax.ShapeDtypeStruct((SUBLANES, nb * sw), F32)] * 2,
            scratch_shapes=[pltpu.VMEM((rows, sw2), F32), pltpu.VMEM((rows, sw2), F32)],
            compiler_params=_cparams("arbitrary"),
            name="s5_prompt",
        )(*args)
        return z, f_re[:1], f_im[:1]
    assert h0[0].shape == (n_chunks, nb * sw)
    st = pl.BlockSpec((n_chunks, sw), lambda b: (0, b))
    return pl.pallas_call(
        functools.partial(_s5_sample_body, n_chunks=n_chunks),
        grid=(nb,),
        in_specs=in_specs + [st, st],
        out_specs=[z_spec, st, st],
        out_shape=[z_shape] + [jax.ShapeDtypeStruct((n_chunks, nb * sw), F32)] * 2,
        compiler_params=_cparams("arbitrary"),
        name="s5_sample",
    )(*args, *h0)


def _s5_glu_body(z_ref, w_ref, b_ref, o_ref, zb_ref, *, tn):
    j = pl.program_id(1)

    @pl.when(j == 0)
    def _():
        zb_ref[...] = z_ref[...].astype(BF16)

    col = pl.multiple_of(j * tn, tn)
    lin = jnp.dot(zb_ref[...], w_ref[...], preferred_element_type=F32) + b_ref[...]
    o_ref[...] = z_ref[:, pl.ds(col, tn)] * jax.nn.sigmoid(lin)


def s5_glu(z, w_glu, b_glu, tm=512, tn=512):
    t, ws = z.shape
    tm = min(tm, t)
    return pl.pallas_call(
        functools.partial(_s5_glu_body, tn=tn),
        grid=(t // tm, ws // tn),
        in_specs=[pl.BlockSpec((tm, ws), lambda i, j: (i, 0)),
                  pl.BlockSpec((ws, tn), lambda i, j: (0, j)),
                  pl.BlockSpec((1, tn), lambda i, j: (0, j))],
        out_specs=pl.BlockSpec((tm, tn), lambda i, j: (i, j)),
        out_shape=jax.ShapeDtypeStruct((t, ws), F32),
        scratch_shapes=[pltpu.VMEM((tm, ws), BF16)],
        compiler_params=_cparams("arbitrary", "arbitrary"),
        name="s5_glu",
    )(z, w_glu, b_glu.reshape(1, ws))


def _layer(x, mod, attend, h0, lw, coef, dims):
    d_model, attn_w, ssm_w = dims
    h = norm_mod(x, lw["g_norm1"], mod, 1, 0)
    qkvu = inproj(h, lw["w_in"], lw["g_q"], lw["g_k"], attn_w)
    o_attn = attend(qkvu)
    z, f_re, f_im = s5_core(qkvu, coef, lw["s5_d"], ssm_w, h0)
    o_ssm = s5_glu(z, lw["w_glu"], lw["b_glu"])
    mix = rms2(o_attn, o_ssm, lw["g_attn_out"], lw["g_ssm_out"])
    x1 = matmul_res(mix, lw["w_out"], x, mod, 2, tm=1024, tn=512)
    h2 = norm_mod(x1, lw["g_norm2"], mod, 4, 3)
    hid = gateup(h2, lw["w_gate"], lw["w_up"])
    y = matmul_res(hid, lw["w_down"], x1, mod, 5, tm=512, tn=256)
    k = qkvu[:, attn_w:2 * attn_w]
    v = qkvu[:, 2 * attn_w:3 * attn_w]
    return y, k, v, f_re, f_im


def kernel(x_prompt, x_sample, cache_k, cache_v, state_s5_re, state_s5_im, page_table, c_prompt, c_sample, w_ada, b_ada, g_norm1, w_in, g_q, g_k, s5_lam_re, s5_lam_im, s5_log_step, s5_b_re, s5_b_im, s5_c_re, s5_c_im, s5_d, w_glu, b_glu, g_attn_out, g_ssm_out, w_out, g_norm2, w_gate, w_up, w_down):
    depth = w_ada.shape[0]
    batch, seq, d_model = x_prompt.shape
    dec_batch, dec_seq, _ = x_sample.shape
    n_pool, n_heads, head_dim = cache_k.shape[1], cache_k.shape[3], cache_k.shape[4]
    assert head_dim == HEAD_DIM and cache_k.shape[2] == PAGE_SIZE and batch == 1
    assert dec_seq == S5_CHUNK
    attn_w = n_heads * head_dim
    ssm_w = d_model - attn_w
    n_groups, n_state = state_s5_re.shape[2], state_s5_re.shape[3]
    past_len = page_table.shape[1] * PAGE_SIZE
    dims = (d_model, attn_w, ssm_w)
    ck = cache_k.reshape(depth, n_pool, PAGE_SIZE, attn_w)
    cv = cache_v.reshape(depth, n_pool, PAGE_SIZE, attn_w)

    y_p = x_prompt.reshape(batch * seq, d_model)
    y_s = x_sample.reshape(dec_batch * dec_seq, d_model)
    outs = [[] for _ in range(8)]
    for l in range(depth):
        lw = dict(g_norm1=g_norm1[l], w_in=w_in[l].astype(BF16), g_q=g_q[l], g_k=g_k[l],
                  s5_d=s5_d[l], w_glu=w_glu[l].astype(BF16), b_glu=b_glu[l],
                  g_attn_out=g_attn_out[l], g_ssm_out=g_ssm_out[l], w_out=w_out[l].astype(BF16),
                  g_norm2=g_norm2[l], w_gate=w_gate[l].astype(BF16), w_up=w_up[l].astype(BF16),
                  w_down=w_down[l].astype(BF16))
        coef = s5_coef(s5_lam_re[l], s5_lam_im[l], s5_log_step[l], s5_b_re[l], s5_b_im[l],
                       s5_c_re[l], s5_c_im[l])
        pad = (-(dec_batch + batch)) % SUBLANES
        c_all = jnp.concatenate([c_sample, c_prompt, jnp.zeros((pad, d_model), F32)], axis=0)
        mod = adaln(c_all, w_ada[l], b_ada[l])
        mod_p = mod[dec_batch:dec_batch + 1]
        mod_s = jnp.repeat(mod[:dec_batch], dec_seq, axis=0)
        h0 = (state_s5_re[l].reshape(dec_batch, n_groups * n_state),
              state_s5_im[l].reshape(dec_batch, n_groups * n_state))

        y_p, kp, vp, hrp, hip = _layer(
            y_p, mod_p, functools.partial(moba_prompt, n_heads=n_heads), None, lw, coef, dims)
        y_s, ks, vs, hrs, his = _layer(
            y_s, mod_s,
            functools.partial(moba_sample, cache_k=ck, cache_v=cv, layer=l, page_table=page_table,
                              n_heads=n_heads, t_new=dec_seq, past_len=past_len),
            h0, lw, coef, dims)
        vals = (kp.reshape(batch, seq, n_heads, head_dim), vp.reshape(batch, seq, n_heads, head_dim),
                hrp.reshape(batch, n_groups, n_state), hip.reshape(batch, n_groups, n_state),
                ks.reshape(dec_batch, dec_seq, n_heads, head_dim),
                vs.reshape(dec_batch, dec_seq, n_heads, head_dim),
                hrs.reshape(dec_batch, n_groups, n_state), his.reshape(dec_batch, n_groups, n_state))
        for o, val in zip(outs, vals):
            o.append(val)
    return (y_p.reshape(batch, seq, d_model), y_s.reshape(dec_batch, dec_seq, d_model),
            *[jnp.stack(o) for o in outs])
```

```python
import functools

import jax
import jax.numpy as jnp
from jax import lax
from jax.experimental import pallas as pl
from jax.experimental.pallas import tpu as pltpu

F32 = jnp.float32
BF16 = jnp.bfloat16

HEAD_DIM = 128
MOBA_BLOCK = 256
MOBA_TOPK = 3
PAST_UNROLL = 4
PAGE_SIZE = 128
S5_GROUP = 16
S5_STATE = 64
S5_CHUNK = 8
EPS = 1e-6
NEG = -1e30

LANES = 128
SUBLANES = 8
S5_LB = LANES // S5_GROUP
VMEM_LIMIT = 56 * 1024 * 1024


def _cparams(*sem):
    return pltpu.CompilerParams(dimension_semantics=sem, vmem_limit_bytes=VMEM_LIMIT)


def _adaln_body(c_ref, w_ref, b_ref, o_ref):
    c = c_ref[...]
    s = (c * jax.nn.sigmoid(c)).astype(BF16)
    o_ref[...] = jnp.dot(s, w_ref[...].astype(BF16), preferred_element_type=F32) + b_ref[...]


def adaln(c, w_ada, b_ada, tn=512):
    m, d = c.shape
    n = w_ada.shape[1]
    return pl.pallas_call(
        _adaln_body,
        grid=(n // tn,),
        in_specs=[pl.BlockSpec((m, d), lambda j: (0, 0)),
                  pl.BlockSpec((d, tn), lambda j: (0, j)),
                  pl.BlockSpec((1, tn), lambda j: (0, j))],
        out_specs=pl.BlockSpec((m, tn), lambda j: (0, j)),
        out_shape=jax.ShapeDtypeStruct((m, n), F32),
        compiler_params=_cparams("arbitrary"),
        name="adaln",
    )(c, w_ada, b_ada.reshape(1, n))


def _mod_spec(mod, tm, tn, chunk, n_col_blocks):
    if mod.shape[0] == 1:
        return pl.BlockSpec((1, tn), lambda i, j: (0, chunk * n_col_blocks + j))
    return pl.BlockSpec((tm, tn), lambda i, j: (i, chunk * n_col_blocks + j))


def _norm_mod_body(x_ref, g_ref, sc_ref, sh_ref, o_ref):
    x = x_ref[...]
    r = lax.rsqrt(jnp.mean(x * x, axis=-1, keepdims=True) + EPS)
    h = ((x * r) * g_ref[...]) * (1.0 + sc_ref[...]) + sh_ref[...]
    o_ref[...] = h.astype(BF16)


def norm_mod(x, g, mod, sc_chunk, sh_chunk, tm=256):
    t, d = x.shape
    tm = min(tm, t)
    return pl.pallas_call(
        _norm_mod_body,
        grid=(t // tm, 1),
        in_specs=[pl.BlockSpec((tm, d), lambda i, j: (i, 0)),
                  pl.BlockSpec((1, d), lambda i, j: (0, 0)),
                  _mod_spec(mod, tm, d, sc_chunk, 1),
                  _mod_spec(mod, tm, d, sh_chunk, 1)],
        out_specs=pl.BlockSpec((tm, d), lambda i, j: (i, 0)),
        out_shape=jax.ShapeDtypeStruct((t, d), BF16),
        compiler_params=_cparams("arbitrary", "arbitrary"),
        name="norm_mod",
    )(x, g.reshape(1, d), mod, mod)


def _inproj_body(h_ref, w_ref, gq_ref, gk_ref, o_ref, *, tn, attn_width):
    j = pl.program_id(1)
    acc = jnp.dot(h_ref[...], w_ref[...], preferred_element_type=F32)
    col0 = j * tn

    @pl.when(col0 < 2 * attn_width)
    def _():
        g = jnp.where(col0 < attn_width, gq_ref[...], gk_ref[...])
        for s in range(tn // HEAD_DIM):
            a = acc[:, s * HEAD_DIM:(s + 1) * HEAD_DIM]
            r = lax.rsqrt(jnp.mean(a * a, axis=-1, keepdims=True) + EPS)
            o_ref[:, s * HEAD_DIM:(s + 1) * HEAD_DIM] = (a * r) * g

    @pl.when(col0 >= 2 * attn_width)
    def _():
        o_ref[...] = acc


def inproj(h, w_in, g_q, g_k, attn_width, tm=1024, tn=512):
    t, d = h.shape
    n = w_in.shape[1]
    tm = min(tm, t)
    assert attn_width % tn == 0
    return pl.pallas_call(
        functools.partial(_inproj_body, tn=tn, attn_width=attn_width),
        grid=(t // tm, n // tn),
        in_specs=[pl.BlockSpec((tm, d), lambda i, j: (i, 0)),
                  pl.BlockSpec((d, tn), lambda i, j: (0, j)),
                  pl.BlockSpec((1, HEAD_DIM), lambda i, j: (0, 0)),
                  pl.BlockSpec((1, HEAD_DIM), lambda i, j: (0, 0))],
        out_specs=pl.BlockSpec((tm, tn), lambda i, j: (i, j)),
        out_shape=jax.ShapeDtypeStruct((t, n), F32),
        compiler_params=_cparams("arbitrary", "arbitrary"),
        name="inproj",
    )(h, w_in, g_q.reshape(1, HEAD_DIM), g_k.reshape(1, HEAD_DIM))


def _matmul_res_body(a_ref, w_ref, res_ref, gt_ref, o_ref):
    acc = jnp.dot(a_ref[...], w_ref[...], preferred_element_type=F32)
    o_ref[...] = res_ref[...] + gt_ref[...] * acc


def matmul_res(a, w, res, mod, gt_chunk, tm, tn):
    t, k = a.shape
    n = w.shape[1]
    tm = min(tm, t)
    return pl.pallas_call(
        _matmul_res_body,
        grid=(t // tm, n // tn),
        in_specs=[pl.BlockSpec((tm, k), lambda i, j: (i, 0)),
                  pl.BlockSpec((k, tn), lambda i, j: (0, j)),
                  pl.BlockSpec((tm, tn), lambda i, j: (i, j)),
                  _mod_spec(mod, tm, tn, gt_chunk, n // tn)],
        out_specs=pl.BlockSpec((tm, tn), lambda i, j: (i, j)),
        out_shape=jax.ShapeDtypeStruct((t, n), F32),
        compiler_params=_cparams("arbitrary", "arbitrary"),
        name="matmul_res",
    )(a, w, res, mod)


def _gateup_body(h_ref, wg_ref, wu_ref, o_ref):
    h = h_ref[...]
    g = jnp.dot(h, wg_ref[...], preferred_element_type=F32)
    u = jnp.dot(h, wu_ref[...], preferred_element_type=F32)
    o_ref[...] = ((g * jax.nn.sigmoid(g)) * u).astype(BF16)


def gateup(h, w_gate, w_up, tm=1024, tn=256):
    t, d = h.shape
    n = w_gate.shape[1]
    tm = min(tm, t)
    return pl.pallas_call(
        _gateup_body,
        grid=(t // tm, n // tn),
        in_specs=[pl.BlockSpec((tm, d), lambda i, j: (i, 0)),
                  pl.BlockSpec((d, tn), lambda i, j: (0, j)),
                  pl.BlockSpec((d, tn), lambda i, j: (0, j))],
        out_specs=pl.BlockSpec((tm, tn), lambda i, j: (i, j)),
        out_shape=jax.ShapeDtypeStruct((t, n), BF16),
        compiler_params=_cparams("arbitrary", "arbitrary"),
        name="gateup",
    )(h, w_gate, w_up)


def _rms2_body(a_ref, s_ref, ga_ref, gs_ref, o_ref, *, wa):
    a = a_ref[...]
    ra = lax.rsqrt(jnp.mean(a * a, axis=-1, keepdims=True) + EPS)
    o_ref[:, :wa] = ((a * ra) * ga_ref[...]).astype(BF16)
    s = s_ref[...]
    rs = lax.rsqrt(jnp.mean(s * s, axis=-1, keepdims=True) + EPS)
    o_ref[:, wa:] = ((s * rs) * gs_ref[...]).astype(BF16)


def rms2(o_attn, o_ssm, g_a, g_s, tm=256):
    t, wa = o_attn.shape
    ws = o_ssm.shape[1]
    tm = min(tm, t)
    return pl.pallas_call(
        functools.partial(_rms2_body, wa=wa),
        grid=(t // tm,),
        in_specs=[pl.BlockSpec((tm, wa), lambda i: (i, 0)),
                  pl.BlockSpec((tm, ws), lambda i: (i, 0)),
                  pl.BlockSpec((1, wa), lambda i: (0, 0)),
                  pl.BlockSpec((1, ws), lambda i: (0, 0))],
        out_specs=pl.BlockSpec((tm, wa + ws), lambda i: (i, 0)),
        out_shape=jax.ShapeDtypeStruct((t, wa + ws), BF16),
        compiler_params=_cparams("arbitrary"),
        name="rms2",
    )(o_attn, o_ssm, g_a.reshape(1, wa), g_s.reshape(1, ws))


def _top_mask_rows(s, n_avail, n_sel):
    row = lax.broadcasted_iota(jnp.int32, s.shape, 0)
    s = jnp.where(row < n_avail, s, -jnp.inf)
    sel = jnp.zeros(s.shape, F32)
    for _ in range(n_sel):
        mx = jnp.max(s, axis=0, keepdims=True)
        first = jnp.min(jnp.where(s == mx, row, s.shape[0]), axis=0, keepdims=True)
        hit = row == first
        sel = jnp.where(hit, 1.0, sel)
        s = jnp.where(hit, -jnp.inf, s)
    return jnp.where(row < n_avail, sel, 0.0)


def _moba_prompt_body(q_ref, k_ref, v_ref, o_ref, kb_ref, vt_ref, km_ref, sel_ref, *, n_blk, n_sel):
    i = pl.program_id(1)
    blk = MOBA_BLOCK

    @pl.when(i == 0)
    def _():
        kb_ref[...] = k_ref[...].astype(BF16)
        km_ref[...] = jnp.zeros(km_ref.shape, F32)
        for b in range(n_blk):
            rows = slice(b * blk, (b + 1) * blk)
            vt_ref[:, rows] = v_ref[rows, :].T.astype(BF16)
            km_ref[b:b + 1, :] = jnp.mean(k_ref[rows, :], axis=0, keepdims=True)

    q_t = q_ref[...].T
    qs_t = (q_t * (HEAD_DIM ** -0.5)).astype(BF16)

    row0 = pl.multiple_of(i * blk, blk)
    s = jnp.dot(kb_ref[pl.ds(row0, blk), :], qs_t, preferred_element_type=F32)
    k_id = lax.broadcasted_iota(jnp.int32, s.shape, 0)
    q_id = lax.broadcasted_iota(jnp.int32, s.shape, 1)
    s = jnp.where(k_id <= q_id, s, NEG)
    m = jnp.max(s, axis=0, keepdims=True)
    p = jnp.exp(s - m)
    l = jnp.sum(p, axis=0, keepdims=True)
    acc = jnp.dot(vt_ref[:, pl.ds(row0, blk)], p.astype(BF16), preferred_element_type=F32)

    if n_sel > 0:
        sc = jnp.dot(km_ref[...], q_t, precision=lax.Precision.HIGHEST, preferred_element_type=F32)
        sel_ref[...] = _top_mask_rows(sc, i, n_sel)

        def group(g, carry):
            m, l, acc = carry
            r0s, ons, ss = [], [], []
            for u in range(PAST_UNROLL):
                j = jnp.minimum(g * PAST_UNROLL + u, n_blk - 1)
                r0 = pl.multiple_of(j * blk, blk)
                r0s.append(r0)
                ons.append(sel_ref[pl.ds(j, 1), :])
                ss.append(jnp.dot(kb_ref[pl.ds(r0, blk), :], qs_t, preferred_element_type=F32))
            for r0, on, s in zip(r0s, ons, ss):
                m_c = jnp.maximum(m, jnp.max(s, axis=0, keepdims=True))
                p = jnp.exp(s - m_c)
                m_n = jnp.where(on > 0.0, m_c, m)
                alpha = jnp.exp(m - m_n)
                l = alpha * l + on * jnp.sum(p, axis=0, keepdims=True)
                pv = jnp.dot(vt_ref[:, pl.ds(r0, blk)], p.astype(BF16), preferred_element_type=F32)
                acc = alpha * acc + on * pv
                m = m_n
            return m, l, acc

        n_trips = (i + PAST_UNROLL - 1) // PAST_UNROLL
        m, l, acc = lax.fori_loop(0, n_trips, group, (m, l, acc))

    o_ref[...] = (acc / l).T


def moba_prompt(qkvu, n_heads):
    s_len = qkvu.shape[0]
    blk = MOBA_BLOCK
    n_blk = s_len // blk
    n_cand = (s_len - 1) // blk
    n_sel = min(MOBA_TOPK, n_cand)
    assert s_len % blk == 0 and n_blk <= LANES
    return pl.pallas_call(
        functools.partial(_moba_prompt_body, n_blk=n_blk, n_sel=n_sel),
        grid=(n_heads, n_blk),
        in_specs=[pl.BlockSpec((blk, HEAD_DIM), lambda h, i: (i, h)),
                  pl.BlockSpec((s_len, HEAD_DIM), lambda h, i: (0, n_heads + h)),
                  pl.BlockSpec((s_len, HEAD_DIM), lambda h, i: (0, 2 * n_heads + h))],
        out_specs=pl.BlockSpec((blk, HEAD_DIM), lambda h, i: (i, h)),
        out_shape=jax.ShapeDtypeStruct((s_len, n_heads * HEAD_DIM), F32),
        scratch_shapes=[pltpu.VMEM((s_len, HEAD_DIM), BF16),
                        pltpu.VMEM((HEAD_DIM, s_len), BF16),
                        pltpu.VMEM((LANES, HEAD_DIM), F32),
                        pltpu.VMEM((LANES, blk), F32)],
        compiler_params=_cparams("arbitrary", "arbitrary"),
        name="moba_prompt",
    )(qkvu, qkvu, qkvu)


def _head_sums(x, n_heads):
    parts = []
    for h in range(n_heads):
        sm = jnp.sum(x[:, h * HEAD_DIM:(h + 1) * HEAD_DIM], axis=-1, keepdims=True)
        parts.append(jnp.broadcast_to(sm, (x.shape[0], HEAD_DIM)))
    return jnp.concatenate(parts, axis=1)


def _expand_stat(col, n_heads, t):
    half = n_heads // 2
    parts = []
    for h in range(n_heads):
        r0 = ((h % half) * 2 + h // half) * t
        parts.append(jnp.broadcast_to(col[r0:r0 + t, :], (t, HEAD_DIM)))
    return jnp.concatenate(parts, axis=1)


def _moba_sample_body(pt_ref, q_ref, kn_ref, vn_ref, ka_ref, kb_ref, va_ref, vb_ref, o_ref,
                      m_ref, l_ref, acc_ref, ks_ref, *, n_heads, n_blk, n_sel, t_new):
    del pt_ref
    b = pl.program_id(1)
    w = n_heads * HEAD_DIM
    half = n_heads // 2
    nt = (((1,), (1,)), ((), ()))

    q = q_ref[...]
    qs = q * (HEAD_DIM ** -0.5)

    def pair_rows(pa_ref, pb_ref, c):
        return jnp.concatenate([pa_ref[pl.ds(c, 2 * PAGE_SIZE, stride=half), :],
                                pb_ref[pl.ds(c, 2 * PAGE_SIZE, stride=half), :]], axis=0)

    n_key = 2 * MOBA_BLOCK
    row_head = lax.broadcasted_iota(jnp.int32, (2 * t_new, n_key), 0) // t_new
    key_head = lax.broadcasted_iota(jnp.int32, (2 * t_new, n_key), 1) % 2
    own = row_head == key_head
    sub_odd = lax.broadcasted_iota(jnp.int32, (SUBLANES, HEAD_DIM), 0) % 2 == 1
    ksum = [None] * n_heads
    s_parts = []
    for c in range(half):
        kp = pair_rows(ka_ref, kb_ref, c)
        tile = jnp.sum(kp.reshape(n_key // SUBLANES, SUBLANES, HEAD_DIM), axis=0)
        ksum[c] = jnp.sum(jnp.where(sub_odd, 0.0, tile), axis=0, keepdims=True)
        ksum[c + half] = jnp.sum(jnp.where(sub_odd, tile, 0.0), axis=0, keepdims=True)
        q2 = jnp.concatenate([qs[:, c * HEAD_DIM:(c + 1) * HEAD_DIM],
                              qs[:, (c + half) * HEAD_DIM:(c + half + 1) * HEAD_DIM]], axis=0)
        sc2 = lax.dot_general(q2.astype(BF16), kp.astype(BF16), nt, preferred_element_type=F32)
        s_parts.append(jnp.where(own, sc2, NEG))
    ks_ref[b] = jnp.broadcast_to(jnp.concatenate(ksum, axis=1), (t_new, w))
    s = jnp.concatenate(s_parts, axis=0)
    m = jnp.max(s, axis=-1, keepdims=True)
    p = jnp.exp(s - m)
    l = jnp.sum(p, axis=-1, keepdims=True)
    pb = p.astype(BF16)
    o_parts = [None] * n_heads
    for c in range(half):
        vp = pair_rows(va_ref, vb_ref, c).astype(BF16)
        r = jnp.dot(pb[c * 2 * t_new:(c + 1) * 2 * t_new, :], vp, preferred_element_type=F32)
        o_parts[c] = r[:t_new]
        o_parts[c + half] = r[t_new:]
    acc_ref[b] = jnp.concatenate(o_parts, axis=1)
    m_ref[b] = _expand_stat(m, n_heads, t_new)
    l_ref[b] = _expand_stat(l, n_heads, t_new)

    @pl.when(b == n_blk - 1)
    def _():
        sc = [_head_sums(q * (ks_ref[j] * (1.0 / MOBA_BLOCK)), n_heads) for j in range(n_blk)]
        sel = [jnp.zeros((t_new, w), F32) for _ in range(n_blk)]
        for _ in range(n_sel):
            mx = sc[0]
            for j in range(1, n_blk):
                mx = jnp.maximum(mx, sc[j])
            taken = jnp.zeros((t_new, w), F32)
            for j in range(n_blk):
                hit = jnp.logical_and(sc[j] == mx, taken == 0.0)
                sel[j] = jnp.where(hit, 1.0, sel[j])
                sc[j] = jnp.where(hit, -jnp.inf, sc[j])
                taken = jnp.where(hit, 1.0, taken)

        kn = kn_ref[...]
        vn = vn_ref[...]
        t_id = lax.broadcasted_iota(jnp.int32, (t_new, w), 0)
        s_own = []
        for tk in range(t_new):
            so = _head_sums(qs * kn[tk:tk + 1, :], n_heads)
            s_own.append(jnp.where(t_id >= tk, so, NEG))
        m_all = s_own[0]
        for tk in range(1, t_new):
            m_all = jnp.maximum(m_all, s_own[tk])
        for j in range(n_blk):
            m_all = jnp.maximum(m_all, jnp.where(sel[j] > 0.0, m_ref[j], NEG))
        num = jnp.zeros((t_new, w), F32)
        den = jnp.zeros((t_new, w), F32)
        for j in range(n_blk):
            wj = sel[j] * jnp.exp(jnp.where(sel[j] > 0.0, m_ref[j], NEG) - m_all)
            num = num + wj * acc_ref[j]
            den = den + wj * l_ref[j]
        for tk in range(t_new):
            pk = jnp.exp(s_own[tk] - m_all)
            num = num + pk * vn[tk:tk + 1, :]
            den = den + pk
        o_ref[...] = num / den


def moba_sample(qkvu, cache_k, cache_v, layer, page_table, n_heads, t_new, past_len):
    n_seq = page_table.shape[0]
    w = n_heads * HEAD_DIM
    ppb = MOBA_BLOCK // PAGE_SIZE
    n_blk = past_len // MOBA_BLOCK
    n_sel = min(MOBA_TOPK, n_blk)
    assert past_len % MOBA_BLOCK == 0 and n_blk >= 1 and ppb == 2 and t_new <= MOBA_BLOCK
    n_pages = page_table.shape[1]
    pt = page_table.reshape(-1)

    def page_spec(off):
        return pl.BlockSpec((None, None, PAGE_SIZE * n_heads, HEAD_DIM),
                            lambda s, b, pt: (layer, pt[s * n_pages + b * ppb + off], 0, 0))

    return pl.pallas_call(
        functools.partial(_moba_sample_body, n_heads=n_heads, n_blk=n_blk, n_sel=n_sel, t_new=t_new),
        grid_spec=pltpu.PrefetchScalarGridSpec(
            num_scalar_prefetch=1,
            grid=(n_seq, n_blk),
            in_specs=[pl.BlockSpec((t_new, w), lambda s, b, pt: (s, 0)),
                      pl.BlockSpec((t_new, w), lambda s, b, pt: (s, 1)),
                      pl.BlockSpec((t_new, w), lambda s, b, pt: (s, 2)),
                      page_spec(0), page_spec(1), page_spec(0), page_spec(1)],
            out_specs=pl.BlockSpec((t_new, w), lambda s, b, pt: (s, 0)),
            scratch_shapes=[pltpu.VMEM((n_blk, t_new, w), F32),
                            pltpu.VMEM((n_blk, t_new, w), F32),
                            pltpu.VMEM((n_blk, t_new, w), F32),
                            pltpu.VMEM((n_blk, t_new, w), F32)]),
        out_shape=jax.ShapeDtypeStruct((n_seq * t_new, w), F32),
        compiler_params=_cparams("arbitrary", "arbitrary"),
        name="moba_sample",
    )(pt, qkvu, qkvu, qkvu, cache_k, cache_k, cache_v, cache_v)


def _cpowers(lr, li, step, n):
    mag = jnp.exp(lr * step)
    a_re, a_im = mag * jnp.cos(li * step), mag * jnp.sin(li * step)
    out = [(jnp.ones_like(a_re), jnp.zeros_like(a_im))]
    for _ in range(n):
        p_re, p_im = out[-1]
        out.append((p_re * a_re - p_im * a_im, p_re * a_im + p_im * a_re))
    return out


def _zoh_gain(a_re, a_im, lr, li):
    den = lr * lr + li * li
    f_re = ((a_re - 1.0) * lr + a_im * li) / den
    f_im = (a_im * lr - (a_re - 1.0) * li) / den
    return f_re, f_im


def _s5_coef_body(lrr_ref, lir_ref, lsr_ref, lrc_ref, lic_ref, lsc_ref,
                  btr_ref, bti_ref, ctr_ref, cti_ref, w_ref, mv_ref, al_ref):
    gc, ch, ns = S5_GROUP, S5_CHUNK, S5_STATE
    sw = S5_LB * ns
    aw = ch * LANES
    lr, li = lrr_ref[0], lir_ref[0]
    pw = _cpowers(lr, li, jnp.exp(lsr_ref[0]), ch)
    f_re, f_im = _zoh_gain(pw[1][0], pw[1][1], lr, li)
    bt_re, bt_im = btr_ref[0], bti_ref[0]
    row = lax.broadcasted_iota(jnp.int32, (LANES, sw), 0)
    lane = lax.broadcasted_iota(jnp.int32, (LANES, sw), 1)
    same = (row // gc) == (lane // ns)
    bb_re = jnp.where(same, f_re * bt_re - f_im * bt_im, 0.0)
    bb_im = jnp.where(same, f_re * bt_im + f_im * bt_re, 0.0)
    for j in range(ch):
        p_re, p_im = pw[ch - 1 - j]
        w_ref[0, j * LANES:(j + 1) * LANES, :] = jnp.concatenate(
            [p_re * bb_re - p_im * bb_im, p_re * bb_im + p_im * bb_re], axis=1).astype(BF16)
    al_re, al_im = pw[ch]
    al_ref[0] = jnp.concatenate([jnp.broadcast_to(al_re, (SUBLANES, sw)),
                                 jnp.broadcast_to(al_im, (SUBLANES, sw))], axis=1)

    lrc, lic = lrc_ref[0], lic_ref[0]
    qw = _cpowers(lrc, lic, jnp.exp(lsc_ref[0]), ch)
    row2 = lax.broadcasted_iota(jnp.int32, (sw, LANES), 0)
    lane2 = lax.broadcasted_iota(jnp.int32, (sw, LANES), 1)
    same2 = (row2 // ns) == (lane2 // gc)
    ct_re = jnp.where(same2, ctr_ref[0], 0.0)
    ct_im = jnp.where(same2, cti_ref[0], 0.0)
    hp = lax.Precision.HIGHEST
    mv_ref[0, :aw, :] = jnp.zeros((aw, aw), BF16)
    for e in range(ch + 1):
        q_re, q_im = qw[e]
        e_re = q_re * ct_re - q_im * ct_im
        e_im = q_re * ct_im + q_im * ct_re
        if e < ch:
            k_e = (jnp.dot(bb_re, e_re, precision=hp, preferred_element_type=F32)
                   - jnp.dot(bb_im, e_im, precision=hp, preferred_element_type=F32)).astype(BF16)
            for j in range(ch - e):
                t = j + e
                mv_ref[0, j * LANES:(j + 1) * LANES, t * LANES:(t + 1) * LANES] = k_e
        if e >= 1:
            t = e - 1
            mv_ref[0, aw:aw + sw, t * LANES:(t + 1) * LANES] = e_re.astype(BF16)
            mv_ref[0, aw + sw:, t * LANES:(t + 1) * LANES] = (-e_im).astype(BF16)


def s5_coef(lam_re, lam_im, log_step, b_re, b_im, c_re, c_im):
    g, p = lam_re.shape
    gc, ch = S5_GROUP, S5_CHUNK
    assert p == S5_STATE and b_re.shape == (g, p, gc) and g % S5_LB == 0
    nb = g // S5_LB
    sw, aw = S5_LB * p, ch * LANES
    row = lambda a: a.reshape(nb, 1, sw)
    col = lambda a: a.reshape(nb, sw, 1)
    ls = jnp.broadcast_to(log_step[:, None], (g, p))

    def bt(b):
        x = b.reshape(nb, S5_LB, p, gc).transpose(0, 3, 1, 2).reshape(nb, 1, gc, sw)
        return jnp.broadcast_to(x, (nb, S5_LB, gc, sw)).reshape(nb, LANES, sw)

    def ct(c):
        x = c.reshape(nb, S5_LB, gc, p).transpose(0, 1, 3, 2).reshape(nb, sw, 1, gc)
        return jnp.broadcast_to(x, (nb, sw, S5_LB, gc)).reshape(nb, sw, LANES)

    vec_r = pl.BlockSpec((1, 1, sw), lambda i: (i, 0, 0))
    vec_c = pl.BlockSpec((1, sw, 1), lambda i: (i, 0, 0))
    return pl.pallas_call(
        _s5_coef_body,
        grid=(nb,),
        in_specs=[vec_r, vec_r, vec_r, vec_c, vec_c, vec_c,
                  pl.BlockSpec((1, LANES, sw), lambda i: (i, 0, 0)),
                  pl.BlockSpec((1, LANES, sw), lambda i: (i, 0, 0)),
                  pl.BlockSpec((1, sw, LANES), lambda i: (i, 0, 0)),
                  pl.BlockSpec((1, sw, LANES), lambda i: (i, 0, 0))],
        out_specs=[pl.BlockSpec((1, aw, 2 * sw), lambda i: (i, 0, 0)),
                   pl.BlockSpec((1, aw + 2 * sw, aw), lambda i: (i, 0, 0)),
                   pl.BlockSpec((1, SUBLANES, 2 * sw), lambda i: (i, 0, 0))],
        out_shape=[jax.ShapeDtypeStruct((nb, aw, 2 * sw), BF16),
                   jax.ShapeDtypeStruct((nb, aw + 2 * sw, aw), BF16),
                   jax.ShapeDtypeStruct((nb, SUBLANES, 2 * sw), F32)],
        compiler_params=_cparams("arbitrary"),
        name="s5_coef",
    )(row(lam_re), row(lam_im), row(ls), col(lam_re), col(lam_im), col(ls),
      bt(b_re), bt(b_im), ct(c_re), ct(c_im))


def _s5_steps(u_ref, chunk0, rows):
    return [u_ref[pl.ds(chunk0 * S5_CHUNK + j, rows, stride=S5_CHUNK), :] for j in range(S5_CHUNK)]


def _s5_emit(z_ref, y, u, d, chunk0, rows):
    for t in range(S5_CHUNK):
        z = jax.nn.gelu(y[:, t * LANES:(t + 1) * LANES] + d * u[t])
        z_ref[pl.ds(chunk0 * S5_CHUNK + t, rows, stride=S5_CHUNK), :] = z


def _s5_prompt_body(u_ref, d_ref, w_ref, mv_ref, al_ref, z_ref, fr_ref, fi_ref, x_ref, h_ref,
                    *, n_chunks, rows):
    sw = x_ref.shape[1] // 2
    a_re = al_ref[0, 0:1, :sw]
    a_im = al_ref[0, 0:1, sw:]

    def tile(k, carry):
        h_re, h_im = carry
        r0 = pl.multiple_of(k * SUBLANES, SUBLANES)
        x = x_ref[pl.ds(r0, SUBLANES), :]
        rows_re, rows_im = [], []
        for s in range(SUBLANES):
            rows_re.append(h_re)
            rows_im.append(h_im)
            n_re = a_re * h_re - a_im * h_im + x[s:s + 1, :sw]
            n_im = a_re * h_im + a_im * h_re + x[s:s + 1, sw:]
            h_re, h_im = n_re, n_im
        h_ref[pl.ds(r0, SUBLANES), :] = jnp.concatenate(
            [jnp.concatenate(rows_re, axis=0), jnp.concatenate(rows_im, axis=0)], axis=1)
        return h_re, h_im

    state = (jnp.zeros((1, sw), F32), jnp.zeros((1, sw), F32))
    for c0 in range(0, n_chunks, rows):
        u = _s5_steps(u_ref, c0, rows)
        a = jnp.concatenate(u, axis=1).astype(BF16)
        x_ref[...] = jnp.dot(a, w_ref[0], preferred_element_type=F32)
        state = lax.fori_loop(0, rows // SUBLANES, tile, state)
        lhs = jnp.concatenate([a, h_ref[...].astype(BF16)], axis=1)
        y = jnp.dot(lhs, mv_ref[0], preferred_element_type=F32)
        _s5_emit(z_ref, y, u, d_ref[...], c0, rows)
    fr_ref[...] = jnp.broadcast_to(state[0], fr_ref.shape)
    fi_ref[...] = jnp.broadcast_to(state[1], fi_ref.shape)


def _s5_sample_body(u_ref, d_ref, w_ref, mv_ref, al_ref, hr_ref, hi_ref, z_ref, fr_ref, fi_ref,
                    *, n_chunks):
    sw = hr_ref.shape[1]
    a_re = al_ref[0, 0:1, :sw]
    a_im = al_ref[0, 0:1, sw:]
    u = _s5_steps(u_ref, 0, n_chunks)
    a = jnp.concatenate(u, axis=1).astype(BF16)
    x = jnp.dot(a, w_ref[0], preferred_element_type=F32)
    h_re, h_im = hr_ref[...], hi_ref[...]
    lhs = jnp.concatenate([a, h_re.astype(BF16), h_im.astype(BF16)], axis=1)
    y = jnp.dot(lhs, mv_ref[0], preferred_element_type=F32)
    _s5_emit(z_ref, y, u, d_ref[...], 0, n_chunks)
    fr_ref[...] = a_re * h_re - a_im * h_im + x[:, :sw]
    fi_ref[...] = a_re * h_im + a_im * h_re + x[:, sw:]


def s5_core(qkvu, coef, d_skip, ssm_width, h0, rows=256):
    w_all, mv_all, al = coef
    nb, aw, sw2 = w_all.shape
    sw = sw2 // 2
    t = qkvu.shape[0]
    n_chunks = t // S5_CHUNK
    assert t % S5_CHUNK == 0 and nb * LANES == ssm_width
    u_blk0 = (qkvu.shape[1] - ssm_width) // LANES
    in_specs = [pl.BlockSpec((t, LANES), lambda b: (0, u_blk0 + b)),
                pl.BlockSpec((1, LANES), lambda b: (0, b)),
                pl.BlockSpec((1, aw, sw2), lambda b: (b, 0, 0)),
                pl.BlockSpec((1, aw + sw2, aw), lambda b: (b, 0, 0)),
                pl.BlockSpec((1, SUBLANES, sw2), lambda b: (b, 0, 0))]
    z_spec = pl.BlockSpec((t, LANES), lambda b: (0, b))
    z_shape = jax.ShapeDtypeStruct((t, ssm_width), F32)
    args = (qkvu, d_skip.reshape(1, ssm_width), w_all, mv_all, al)
    if h0 is None:
        rows = min(rows, n_chunks)
        assert n_chunks % rows == 0 and rows % SUBLANES == 0
        st = pl.BlockSpec((SUBLANES, sw), lambda b: (0, b))
        z, f_re, f_im = pl.pallas_call(
            functools.partial(_s5_prompt_body, n_chunks=n_chunks, rows=rows),
            grid=(nb,),
            in_specs=in_specs,
            out_specs=[z_spec, st, st],
            out_shape=[z_shape] + [jax.ShapeDtypeStruct((SUBLANES, nb * sw), F32)] * 2,
            scratch_shapes=[pltpu.VMEM((rows, sw2), F32), pltpu.VMEM((rows, sw2), F32)],
            compiler_params=_cparams("arbitrary"),
            name="s5_prompt",
        )(*args)
        return z, f_re[:1], f_im[:1]
    assert h0[0].shape == (n_chunks, nb * sw)
    st = pl.BlockSpec((n_chunks, sw), lambda b: (0, b))
    return pl.pallas_call(
        functools.partial(_s5_sample_body, n_chunks=n_chunks),
        grid=(nb,),
        in_specs=in_specs + [st, st],
        out_specs=[z_spec, st, st],
        out_shape=[z_shape] + [jax.ShapeDtypeStruct((n_chunks, nb * sw), F32)] * 2,
        compiler_params=_cparams("arbitrary"),
        name="s5_sample",
    )(*args, *h0)


def _s5_glu_body(z_ref, w_ref, b_ref, o_ref, zb_ref, *, tn):
    j = pl.program_id(1)

    @pl.when(j == 0)
    def _():
        zb_ref[...] = z_ref[...].astype(BF16)

    col = pl.multiple_of(j * tn, tn)
    lin = jnp.dot(zb_ref[...], w_ref[...], preferred_element_type=F32) + b_ref[...]
    o_ref[...] = z_ref[:, pl.ds(col, tn)] * jax.nn.sigmoid(lin)


def s5_glu(z, w_glu, b_glu, tm=512, tn=512):
    t, ws = z.shape
    tm = min(tm, t)
    return pl.pallas_call(
        functools.partial(_s5_glu_body, tn=tn),
        grid=(t // tm, ws // tn),
        in_specs=[pl.BlockSpec((tm, ws), lambda i, j: (i, 0)),
                  pl.BlockSpec((ws, tn), lambda i, j: (0, j)),
                  pl.BlockSpec((1, tn), lambda i, j: (0, j))],
        out_specs=pl.BlockSpec((tm, tn), lambda i, j: (i, j)),
        out_shape=jax.ShapeDtypeStruct((t, ws), F32),
        scratch_shapes=[pltpu.VMEM((tm, ws), BF16)],
        compiler_params=_cparams("arbitrary", "arbitrary"),
        name="s5_glu",
    )(z, w_glu, b_glu.reshape(1, ws))


def _layer(x, mod, attend, h0, lw, coef, dims):
    d_model, attn_w, ssm_w = dims
    h = norm_mod(x, lw["g_norm1"], mod, 1, 0)
    qkvu = inproj(h, lw["w_in"], lw["g_q"], lw["g_k"], attn_w)
    o_attn = attend(qkvu)
    z, f_re, f_im = s5_core(qkvu, coef, lw["s5_d"], ssm_w, h0)
    o_ssm = s5_glu(z, lw["w_glu"], lw["b_glu"])
    mix = rms2(o_attn, o_ssm, lw["g_attn_out"], lw["g_ssm_out"])
    x1 = matmul_res(mix, lw["w_out"], x, mod, 2, tm=1024, tn=512)
    h2 = norm_mod(x1, lw["g_norm2"], mod, 4, 3)
    hid = gateup(h2, lw["w_gate"], lw["w_up"])
    y = matmul_res(hid, lw["w_down"], x1, mod, 5, tm=512, tn=256)
    k = qkvu[:, attn_w:2 * attn_w]
    v = qkvu[:, 2 * attn_w:3 * attn_w]
    return y, k, v, f_re, f_im


def kernel(x_prompt, x_sample, cache_k, cache_v, state_s5_re, state_s5_im, page_table, c_prompt, c_sample, w_ada, b_ada, g_norm1, w_in, g_q, g_k, s5_lam_re, s5_lam_im, s5_log_step, s5_b_re, s5_b_im, s5_c_re, s5_c_im, s5_d, w_glu, b_glu, g_attn_out, g_ssm_out, w_out, g_norm2, w_gate, w_up, w_down):
    depth = w_ada.shape[0]
    batch, seq, d_model = x_prompt.shape
    dec_batch, dec_seq, _ = x_sample.shape
    n_pool, n_heads, head_dim = cache_k.shape[1], cache_k.shape[3], cache_k.shape[4]
    assert head_dim == HEAD_DIM and cache_k.shape[2] == PAGE_SIZE and batch == 1
    assert dec_seq == S5_CHUNK
    attn_w = n_heads * head_dim
    ssm_w = d_model - attn_w
    n_groups, n_state = state_s5_re.shape[2], state_s5_re.shape[3]
    past_len = page_table.shape[1] * PAGE_SIZE
    dims = (d_model, attn_w, ssm_w)
    ck = cache_k.reshape(depth, n_pool, PAGE_SIZE * n_heads, head_dim)
    cv = cache_v.reshape(depth, n_pool, PAGE_SIZE * n_heads, head_dim)

    y_p = x_prompt.reshape(batch * seq, d_model)
    y_s = x_sample.reshape(dec_batch * dec_seq, d_model)
    outs = [[] for _ in range(8)]
    for l in range(depth):
        lw = dict(g_norm1=g_norm1[l], w_in=w_in[l].astype(BF16), g_q=g_q[l], g_k=g_k[l],
                  s5_d=s5_d[l], w_glu=w_glu[l].astype(BF16), b_glu=b_glu[l],
                  g_attn_out=g_attn_out[l], g_ssm_out=g_ssm_out[l], w_out=w_out[l].astype(BF16),
                  g_norm2=g_norm2[l], w_gate=w_gate[l].astype(BF16), w_up=w_up[l].astype(BF16),
                  w_down=w_down[l].astype(BF16))
        coef = s5_coef(s5_lam_re[l], s5_lam_im[l], s5_log_step[l], s5_b_re[l], s5_b_im[l],
                       s5_c_re[l], s5_c_im[l])
        pad = (-(dec_batch + batch)) % SUBLANES
        c_all = jnp.concatenate([c_sample, c_prompt, jnp.zeros((pad, d_model), F32)], axis=0)
        mod = adaln(c_all, w_ada[l], b_ada[l])
        mod_p = mod[dec_batch:dec_batch + 1]
        mod_s = jnp.repeat(mod[:dec_batch], dec_seq, axis=0)
        h0 = (state_s5_re[l].reshape(dec_batch, n_groups * n_state),
              state_s5_im[l].reshape(dec_batch, n_groups * n_state))

        y_p, kp, vp, hrp, hip = _layer(
            y_p, mod_p, functools.partial(moba_prompt, n_heads=n_heads), None, lw, coef, dims)
        y_s, ks, vs, hrs, his = _layer(
            y_s, mod_s,
            functools.partial(moba_sample, cache_k=ck, cache_v=cv, layer=l, page_table=page_table,
                              n_heads=n_heads, t_new=dec_seq, past_len=past_len),
            h0, lw, coef, dims)
        vals = (kp.reshape(batch, seq, n_heads, head_dim), vp.reshape(batch, seq, n_heads, head_dim),
                hrp.reshape(batch, n_groups, n_state), hip.reshape(batch, n_groups, n_state),
                ks.reshape(dec_batch, dec_seq, n_heads, head_dim),
                vs.reshape(dec_batch, dec_seq, n_heads, head_dim),
                hrs.reshape(dec_batch, n_groups, n_state), his.reshape(dec_batch, n_groups, n_state))
        for o, val in zip(outs, vals):
            o.append(val)
    return (y_p.reshape(batch, seq, d_model), y_s.reshape(dec_batch, dec_seq, d_model),
            *[jnp.stack(o) for o in outs])
```

```python
import functools

import jax
import jax.numpy as jnp
from jax import lax
from jax.experimental import pallas as pl
from jax.experimental.pallas import tpu as pltpu

F32 = jnp.float32
BF16 = jnp.bfloat16

HEAD_DIM = 128
MOBA_BLOCK = 256
MOBA_TOPK = 3
PAST_UNROLL = 4
PROMPT_HEADS = 2
SAMPLE_BLOCKS = 2
PAGE_SIZE = 128
S5_GROUP = 16
S5_STATE = 64
S5_CHUNK = 8
EPS = 1e-6
NEG = -1e30
LOG2E = 1.4426950408889634

LANES = 128
SUBLANES = 8
S5_LB = LANES // S5_GROUP
VMEM_LIMIT = 56 * 1024 * 1024


def _cparams(*sem):
    return pltpu.CompilerParams(dimension_semantics=sem, vmem_limit_bytes=VMEM_LIMIT)


def _adaln_body(c_ref, w_ref, b_ref, o_ref):
    c = c_ref[...]
    s = (c * jax.nn.sigmoid(c)).astype(BF16)
    o_ref[...] = jnp.dot(s, w_ref[...].astype(BF16), preferred_element_type=F32) + b_ref[...]


def adaln(c, w_ada, b_ada, tn=512):
    m, d = c.shape
    n = w_ada.shape[1]
    return pl.pallas_call(
        _adaln_body,
        grid=(n // tn,),
        in_specs=[pl.BlockSpec((m, d), lambda j: (0, 0)),
                  pl.BlockSpec((d, tn), lambda j: (0, j)),
                  pl.BlockSpec((1, tn), lambda j: (0, j))],
        out_specs=pl.BlockSpec((m, tn), lambda j: (0, j)),
        out_shape=jax.ShapeDtypeStruct((m, n), F32),
        compiler_params=_cparams("arbitrary"),
        name="adaln",
    )(c, w_ada, b_ada.reshape(1, n))


def _mod_spec(mod, tm, tn, chunk, n_col_blocks):
    if mod.shape[0] == 1:
        return pl.BlockSpec((1, tn), lambda i, j: (0, chunk * n_col_blocks + j))
    return pl.BlockSpec((tm, tn), lambda i, j: (i, chunk * n_col_blocks + j))


def _norm_mod_body(x_ref, g_ref, sc_ref, sh_ref, o_ref):
    x = x_ref[...]
    r = lax.rsqrt(jnp.mean(x * x, axis=-1, keepdims=True) + EPS)
    h = ((x * r) * g_ref[...]) * (1.0 + sc_ref[...]) + sh_ref[...]
    o_ref[...] = h.astype(BF16)


def norm_mod(x, g, mod, sc_chunk, sh_chunk, tm=256):
    t, d = x.shape
    tm = min(tm, t)
    return pl.pallas_call(
        _norm_mod_body,
        grid=(t // tm, 1),
        in_specs=[pl.BlockSpec((tm, d), lambda i, j: (i, 0)),
                  pl.BlockSpec((1, d), lambda i, j: (0, 0)),
                  _mod_spec(mod, tm, d, sc_chunk, 1),
                  _mod_spec(mod, tm, d, sh_chunk, 1)],
        out_specs=pl.BlockSpec((tm, d), lambda i, j: (i, 0)),
        out_shape=jax.ShapeDtypeStruct((t, d), BF16),
        compiler_params=_cparams("arbitrary", "arbitrary"),
        name="norm_mod",
    )(x, g.reshape(1, d), mod, mod)


def _inproj_body(h_ref, w_ref, gq_ref, gk_ref, o_ref, *, tn, attn_width):
    j = pl.program_id(1)
    acc = jnp.dot(h_ref[...], w_ref[...], preferred_element_type=F32)
    col0 = j * tn

    @pl.when(col0 < 2 * attn_width)
    def _():
        g = jnp.where(col0 < attn_width, gq_ref[...], gk_ref[...])
        for s in range(tn // HEAD_DIM):
            a = acc[:, s * HEAD_DIM:(s + 1) * HEAD_DIM]
            r = lax.rsqrt(jnp.mean(a * a, axis=-1, keepdims=True) + EPS)
            o_ref[:, s * HEAD_DIM:(s + 1) * HEAD_DIM] = (a * r) * g

    @pl.when(col0 >= 2 * attn_width)
    def _():
        o_ref[...] = acc


def inproj(h, w_in, g_q, g_k, attn_width, tm=1024, tn=512):
    t, d = h.shape
    n = w_in.shape[1]
    tm = min(tm, t)
    assert attn_width % tn == 0
    return pl.pallas_call(
        functools.partial(_inproj_body, tn=tn, attn_width=attn_width),
        grid=(t // tm, n // tn),
        in_specs=[pl.BlockSpec((tm, d), lambda i, j: (i, 0)),
                  pl.BlockSpec((d, tn), lambda i, j: (0, j)),
                  pl.BlockSpec((1, HEAD_DIM), lambda i, j: (0, 0)),
                  pl.BlockSpec((1, HEAD_DIM), lambda i, j: (0, 0))],
        out_specs=pl.BlockSpec((tm, tn), lambda i, j: (i, j)),
        out_shape=jax.ShapeDtypeStruct((t, n), F32),
        compiler_params=_cparams("arbitrary", "arbitrary"),
        name="inproj",
    )(h, w_in, g_q.reshape(1, HEAD_DIM), g_k.reshape(1, HEAD_DIM))


def _matmul_res_body(a_ref, w_ref, res_ref, gt_ref, o_ref):
    acc = jnp.dot(a_ref[...], w_ref[...], preferred_element_type=F32)
    o_ref[...] = res_ref[...] + gt_ref[...] * acc


def matmul_res(a, w, res, mod, gt_chunk, tm, tn):
    t, k = a.shape
    n = w.shape[1]
    tm = min(tm, t)
    return pl.pallas_call(
        _matmul_res_body,
        grid=(t // tm, n // tn),
        in_specs=[pl.BlockSpec((tm, k), lambda i, j: (i, 0)),
                  pl.BlockSpec((k, tn), lambda i, j: (0, j)),
                  pl.BlockSpec((tm, tn), lambda i, j: (i, j)),
                  _mod_spec(mod, tm, tn, gt_chunk, n // tn)],
        out_specs=pl.BlockSpec((tm, tn), lambda i, j: (i, j)),
        out_shape=jax.ShapeDtypeStruct((t, n), F32),
        compiler_params=_cparams("arbitrary", "arbitrary"),
        name="matmul_res",
    )(a, w, res, mod)


def _gateup_body(h_ref, wg_ref, wu_ref, o_ref):
    h = h_ref[...]
    g = jnp.dot(h, wg_ref[...], preferred_element_type=F32)
    u = jnp.dot(h, wu_ref[...], preferred_element_type=F32)
    o_ref[...] = ((g * jax.nn.sigmoid(g)) * u).astype(BF16)


def gateup(h, w_gate, w_up, tm=1024, tn=256):
    t, d = h.shape
    n = w_gate.shape[1]
    tm = min(tm, t)
    return pl.pallas_call(
        _gateup_body,
        grid=(t // tm, n // tn),
        in_specs=[pl.BlockSpec((tm, d), lambda i, j: (i, 0)),
                  pl.BlockSpec((d, tn), lambda i, j: (0, j)),
                  pl.BlockSpec((d, tn), lambda i, j: (0, j))],
        out_specs=pl.BlockSpec((tm, tn), lambda i, j: (i, j)),
        out_shape=jax.ShapeDtypeStruct((t, n), BF16),
        compiler_params=_cparams("arbitrary", "arbitrary"),
        name="gateup",
    )(h, w_gate, w_up)


def _rms2_body(a_ref, s_ref, ga_ref, gs_ref, o_ref, *, wa):
    a = a_ref[...]
    ra = lax.rsqrt(jnp.mean(a * a, axis=-1, keepdims=True) + EPS)
    o_ref[:, :wa] = ((a * ra) * ga_ref[...]).astype(BF16)
    s = s_ref[...]
    rs = lax.rsqrt(jnp.mean(s * s, axis=-1, keepdims=True) + EPS)
    o_ref[:, wa:] = ((s * rs) * gs_ref[...]).astype(BF16)


def rms2(o_attn, o_ssm, g_a, g_s, tm=256):
    t, wa = o_attn.shape
    ws = o_ssm.shape[1]
    tm = min(tm, t)
    return pl.pallas_call(
        functools.partial(_rms2_body, wa=wa),
        grid=(t // tm,),
        in_specs=[pl.BlockSpec((tm, wa), lambda i: (i, 0)),
                  pl.BlockSpec((tm, ws), lambda i: (i, 0)),
                  pl.BlockSpec((1, wa), lambda i: (0, 0)),
                  pl.BlockSpec((1, ws), lambda i: (0, 0))],
        out_specs=pl.BlockSpec((tm, wa + ws), lambda i: (i, 0)),
        out_shape=jax.ShapeDtypeStruct((t, wa + ws), BF16),
        compiler_params=_cparams("arbitrary"),
        name="rms2",
    )(o_attn, o_ssm, g_a.reshape(1, wa), g_s.reshape(1, ws))


def _top_mask_rows(s, n_avail, n_sel):
    row = lax.broadcasted_iota(jnp.int32, s.shape, 0)
    s = jnp.where(row < n_avail, s, -jnp.inf)
    sel = jnp.zeros(s.shape, F32)
    for _ in range(n_sel):
        mx = jnp.max(s, axis=0, keepdims=True)
        first = jnp.min(jnp.where(s == mx, row, s.shape[0]), axis=0, keepdims=True)
        hit = row == first
        sel = jnp.where(hit, 1.0, sel)
        s = jnp.where(hit, -jnp.inf, s)
    return jnp.where(row < n_avail, sel, 0.0)


def _moba_prompt_body(q_ref, k_ref, v_ref, o_ref, kb_ref, vt_ref, km_ref, sel_ref, *, n_blk, n_sel):
    i = pl.program_id(1)
    blk = MOBA_BLOCK
    heads = range(PROMPT_HEADS)
    cols = [slice(x * HEAD_DIM, (x + 1) * HEAD_DIM) for x in heads]

    @pl.when(i == 0)
    def _():
        km_ref[...] = jnp.zeros(km_ref.shape, F32)
        for x in heads:
            kb_ref[x] = k_ref[:, cols[x]].astype(BF16)
            for b in range(n_blk):
                rows = slice(b * blk, (b + 1) * blk)
                vt_ref[x, :, rows] = v_ref[rows, cols[x]].T.astype(BF16)
                km_ref[x, b:b + 1, :] = jnp.mean(k_ref[rows, cols[x]], axis=0, keepdims=True)

    q_t = [q_ref[:, cols[x]].T for x in heads]
    qs_t = [(q_t[x] * (HEAD_DIM ** -0.5 * LOG2E)).astype(BF16) for x in heads]

    row0 = pl.multiple_of(i * blk, blk)
    k_id = lax.broadcasted_iota(jnp.int32, (blk, blk), 0)
    q_id = lax.broadcasted_iota(jnp.int32, (blk, blk), 1)
    state = []
    for x in heads:
        s = jnp.dot(kb_ref[x, pl.ds(row0, blk), :], qs_t[x], preferred_element_type=F32)
        s = jnp.where(k_id <= q_id, s, NEG)
        m = jnp.max(s, axis=0, keepdims=True)
        p = jnp.exp2(s - m)
        l = jnp.sum(p, axis=0, keepdims=True)
        acc = jnp.dot(vt_ref[x, :, pl.ds(row0, blk)], p.astype(BF16), preferred_element_type=F32)
        state += [m, l, acc]

    if n_sel > 0:
        for x in heads:
            sc = jnp.dot(km_ref[x], q_t[x], precision=lax.Precision.HIGHEST, preferred_element_type=F32)
            sel_ref[x] = _top_mask_rows(sc, i, n_sel)

        def group(g, carry):
            carry = list(carry)
            js = [jnp.minimum(g * PAST_UNROLL + u, n_blk - 1) for u in range(PAST_UNROLL)]
            r0s = [pl.multiple_of(j * blk, blk) for j in js]
            for x in heads:
                m, l, acc = carry[3 * x:3 * x + 3]
                ss = [jnp.dot(kb_ref[x, pl.ds(r0, blk), :], qs_t[x], preferred_element_type=F32)
                      for r0 in r0s]
                for j, r0, s in zip(js, r0s, ss):
                    on = sel_ref[x, pl.ds(j, 1), :]
                    m_c = jnp.maximum(m, jnp.max(s, axis=0, keepdims=True))
                    p = jnp.exp2(s - m_c)
                    m_n = jnp.where(on > 0.0, m_c, m)
                    alpha = jnp.exp2(m - m_n)
                    l = alpha * l + on * jnp.sum(p, axis=0, keepdims=True)
                    pv = jnp.dot(vt_ref[x, :, pl.ds(r0, blk)], p.astype(BF16), preferred_element_type=F32)
                    acc = alpha * acc + on * pv
                    m = m_n
                carry[3 * x:3 * x + 3] = [m, l, acc]
            return tuple(carry)

        n_trips = (i + PAST_UNROLL - 1) // PAST_UNROLL
        state = lax.fori_loop(0, n_trips, group, tuple(state))

    for x in heads:
        m, l, acc = state[3 * x:3 * x + 3]
        o_ref[:, cols[x]] = (acc / l).T


def moba_prompt(qkvu, n_heads):
    s_len = qkvu.shape[0]
    blk = MOBA_BLOCK
    n_blk = s_len // blk
    n_cand = (s_len - 1) // blk
    n_sel = min(MOBA_TOPK, n_cand)
    hp = PROMPT_HEADS
    assert s_len % blk == 0 and n_blk <= LANES and n_heads % hp == 0
    n_hg = n_heads // hp
    return pl.pallas_call(
        functools.partial(_moba_prompt_body, n_blk=n_blk, n_sel=n_sel),
        grid=(n_hg, n_blk),
        in_specs=[pl.BlockSpec((blk, hp * HEAD_DIM), lambda h, i: (i, h)),
                  pl.BlockSpec((s_len, hp * HEAD_DIM), lambda h, i: (0, n_hg + h)),
                  pl.BlockSpec((s_len, hp * HEAD_DIM), lambda h, i: (0, 2 * n_hg + h))],
        out_specs=pl.BlockSpec((blk, hp * HEAD_DIM), lambda h, i: (i, h)),
        out_shape=jax.ShapeDtypeStruct((s_len, n_heads * HEAD_DIM), F32),
        scratch_shapes=[pltpu.VMEM((hp, s_len, HEAD_DIM), BF16),
                        pltpu.VMEM((hp, HEAD_DIM, s_len), BF16),
                        pltpu.VMEM((hp, LANES, HEAD_DIM), F32),
                        pltpu.VMEM((hp, LANES, blk), F32)],
        compiler_params=_cparams("arbitrary", "arbitrary"),
        name="moba_prompt",
    )(qkvu, qkvu, qkvu)


def _head_sums(x, n_heads):
    parts = []
    for h in range(n_heads):
        sm = jnp.sum(x[:, h * HEAD_DIM:(h + 1) * HEAD_DIM], axis=-1, keepdims=True)
        parts.append(jnp.broadcast_to(sm, (x.shape[0], HEAD_DIM)))
    return jnp.concatenate(parts, axis=1)


def _expand_stat(col, n_heads, t):
    half = n_heads // 2
    parts = []
    for h in range(n_heads):
        r0 = ((h % half) * 2 + h // half) * t
        parts.append(jnp.broadcast_to(col[r0:r0 + t, :], (t, HEAD_DIM)))
    return jnp.concatenate(parts, axis=1)


def _moba_sample_body(pt_ref, q_ref, kn_ref, vn_ref, *refs, n_heads, n_blk, n_sel, t_new):
    del pt_ref
    nb = SAMPLE_BLOCKS
    k_pages, v_pages = refs[:2 * nb], refs[2 * nb:4 * nb]
    o_ref, m_ref, l_ref, acc_ref, ks_ref = refs[4 * nb:]
    step = pl.program_id(1)
    for x in range(nb):
        _moba_sample_block(q_ref, k_pages[2 * x], k_pages[2 * x + 1], v_pages[2 * x], v_pages[2 * x + 1],
                           m_ref, l_ref, acc_ref, ks_ref, step * nb + x, n_heads=n_heads, t_new=t_new)

    @pl.when(step == n_blk // nb - 1)
    def _():
        _moba_sample_merge(q_ref, kn_ref, vn_ref, o_ref, m_ref, l_ref, acc_ref, ks_ref,
                           n_heads=n_heads, n_blk=n_blk, n_sel=n_sel, t_new=t_new)


def _moba_sample_block(q_ref, ka_ref, kb_ref, va_ref, vb_ref, m_ref, l_ref, acc_ref, ks_ref, b,
                       *, n_heads, t_new):
    w = n_heads * HEAD_DIM
    half = n_heads // 2
    nt = (((1,), (1,)), ((), ()))

    q = q_ref[...]
    qs = q * (HEAD_DIM ** -0.5)

    def pair_rows(pa_ref, pb_ref, c):
        return jnp.concatenate([pa_ref[pl.ds(c, 2 * PAGE_SIZE, stride=half), :],
                                pb_ref[pl.ds(c, 2 * PAGE_SIZE, stride=half), :]], axis=0)

    n_key = 2 * MOBA_BLOCK
    row_head = lax.broadcasted_iota(jnp.int32, (2 * t_new, n_key), 0) // t_new
    key_head = lax.broadcasted_iota(jnp.int32, (2 * t_new, n_key), 1) % 2
    own = row_head == key_head
    sub_odd = lax.broadcasted_iota(jnp.int32, (SUBLANES, HEAD_DIM), 0) % 2 == 1
    ksum = [None] * n_heads
    s_parts = []
    for c in range(half):
        kp = pair_rows(ka_ref, kb_ref, c)
        tile = jnp.sum(kp.reshape(n_key // SUBLANES, SUBLANES, HEAD_DIM), axis=0)
        ksum[c] = jnp.sum(jnp.where(sub_odd, 0.0, tile), axis=0, keepdims=True)
        ksum[c + half] = jnp.sum(jnp.where(sub_odd, tile, 0.0), axis=0, keepdims=True)
        q2 = jnp.concatenate([qs[:, c * HEAD_DIM:(c + 1) * HEAD_DIM],
                              qs[:, (c + half) * HEAD_DIM:(c + half + 1) * HEAD_DIM]], axis=0)
        sc2 = lax.dot_general(q2.astype(BF16), kp.astype(BF16), nt, preferred_element_type=F32)
        s_parts.append(jnp.where(own, sc2, NEG))
    ks_ref[b] = jnp.broadcast_to(jnp.concatenate(ksum, axis=1), (t_new, w))
    s = jnp.concatenate(s_parts, axis=0)
    m = jnp.max(s, axis=-1, keepdims=True)
    p = jnp.exp(s - m)
    l = jnp.sum(p, axis=-1, keepdims=True)
    pb = p.astype(BF16)
    o_parts = [None] * n_heads
    for c in range(half):
        vp = pair_rows(va_ref, vb_ref, c).astype(BF16)
        r = jnp.dot(pb[c * 2 * t_new:(c + 1) * 2 * t_new, :], vp, preferred_element_type=F32)
        o_parts[c] = r[:t_new]
        o_parts[c + half] = r[t_new:]
    acc_ref[b] = jnp.concatenate(o_parts, axis=1)
    m_ref[b] = _expand_stat(m, n_heads, t_new)
    l_ref[b] = _expand_stat(l, n_heads, t_new)


def _moba_sample_merge(q_ref, kn_ref, vn_ref, o_ref, m_ref, l_ref, acc_ref, ks_ref,
                       *, n_heads, n_blk, n_sel, t_new):
    w = n_heads * HEAD_DIM
    q = q_ref[...]
    qs = q * (HEAD_DIM ** -0.5)
    sc = [_head_sums(q * (ks_ref[j] * (1.0 / MOBA_BLOCK)), n_heads) for j in range(n_blk)]
    sel = [jnp.zeros((t_new, w), F32) for _ in range(n_blk)]
    for _ in range(n_sel):
        mx = sc[0]
        for j in range(1, n_blk):
            mx = jnp.maximum(mx, sc[j])
        taken = jnp.zeros((t_new, w), F32)
        for j in range(n_blk):
            hit = jnp.logical_and(sc[j] == mx, taken == 0.0)
            sel[j] = jnp.where(hit, 1.0, sel[j])
            sc[j] = jnp.where(hit, -jnp.inf, sc[j])
            taken = jnp.where(hit, 1.0, taken)

    kn = kn_ref[...]
    vn = vn_ref[...]
    t_id = lax.broadcasted_iota(jnp.int32, (t_new, w), 0)
    s_own = []
    for tk in range(t_new):
        so = _head_sums(qs * kn[tk:tk + 1, :], n_heads)
        s_own.append(jnp.where(t_id >= tk, so, NEG))
    m_all = s_own[0]
    for tk in range(1, t_new):
        m_all = jnp.maximum(m_all, s_own[tk])
    for j in range(n_blk):
        m_all = jnp.maximum(m_all, jnp.where(sel[j] > 0.0, m_ref[j], NEG))
    num = jnp.zeros((t_new, w), F32)
    den = jnp.zeros((t_new, w), F32)
    for j in range(n_blk):
        wj = sel[j] * jnp.exp(jnp.where(sel[j] > 0.0, m_ref[j], NEG) - m_all)
        num = num + wj * acc_ref[j]
        den = den + wj * l_ref[j]
    for tk in range(t_new):
        pk = jnp.exp(s_own[tk] - m_all)
        num = num + pk * vn[tk:tk + 1, :]
        den = den + pk
    o_ref[...] = num / den


def moba_sample(qkvu, cache_k, cache_v, layer, page_table, n_heads, t_new, past_len):
    n_seq = page_table.shape[0]
    w = n_heads * HEAD_DIM
    ppb = MOBA_BLOCK // PAGE_SIZE
    n_blk = past_len // MOBA_BLOCK
    n_sel = min(MOBA_TOPK, n_blk)
    nb = SAMPLE_BLOCKS
    assert past_len % MOBA_BLOCK == 0 and n_blk % nb == 0 and ppb == 2 and t_new <= MOBA_BLOCK
    n_pages = page_table.shape[1]
    pt = page_table.reshape(-1)

    def page_spec(page):
        return pl.BlockSpec((None, None, PAGE_SIZE * n_heads, HEAD_DIM),
                            lambda s, b, pt: (layer, pt[s * n_pages + b * nb * ppb + page], 0, 0))

    page_specs = [page_spec(page) for page in range(nb * ppb)]
    return pl.pallas_call(
        functools.partial(_moba_sample_body, n_heads=n_heads, n_blk=n_blk, n_sel=n_sel, t_new=t_new),
        grid_spec=pltpu.PrefetchScalarGridSpec(
            num_scalar_prefetch=1,
            grid=(n_seq, n_blk // nb),
            in_specs=[pl.BlockSpec((t_new, w), lambda s, b, pt: (s, 0)),
                      pl.BlockSpec((t_new, w), lambda s, b, pt: (s, 1)),
                      pl.BlockSpec((t_new, w), lambda s, b, pt: (s, 2))] + page_specs + page_specs,
            out_specs=pl.BlockSpec((t_new, w), lambda s, b, pt: (s, 0)),
            scratch_shapes=[pltpu.VMEM((n_blk, t_new, w), F32),
                            pltpu.VMEM((n_blk, t_new, w), F32),
                            pltpu.VMEM((n_blk, t_new, w), F32),
                            pltpu.VMEM((n_blk, t_new, w), F32)]),
        out_shape=jax.ShapeDtypeStruct((n_seq * t_new, w), F32),
        compiler_params=_cparams("arbitrary", "arbitrary"),
        name="moba_sample",
    )(pt, qkvu, qkvu, qkvu, *([cache_k] * (nb * ppb)), *([cache_v] * (nb * ppb)))


def _cpowers(lr, li, step, n):
    mag = jnp.exp(lr * step)
    a_re, a_im = mag * jnp.cos(li * step), mag * jnp.sin(li * step)
    out = [(jnp.ones_like(a_re), jnp.zeros_like(a_im))]
    for _ in range(n):
        p_re, p_im = out[-1]
        out.append((p_re * a_re - p_im * a_im, p_re * a_im + p_im * a_re))
    return out


def _zoh_gain(a_re, a_im, lr, li):
    den = lr * lr + li * li
    f_re = ((a_re - 1.0) * lr + a_im * li) / den
    f_im = (a_im * lr - (a_re - 1.0) * li) / den
    return f_re, f_im


def _s5_coef_body(lrr_ref, lir_ref, lsr_ref, lrc_ref, lic_ref, lsc_ref,
                  btr_ref, bti_ref, ctr_ref, cti_ref, w_ref, mv_ref, al_ref):
    gc, ch, ns = S5_GROUP, S5_CHUNK, S5_STATE
    sw = S5_LB * ns
    aw = ch * LANES
    lr, li = lrr_ref[0], lir_ref[0]
    pw = _cpowers(lr, li, jnp.exp(lsr_ref[0]), ch)
    f_re, f_im = _zoh_gain(pw[1][0], pw[1][1], lr, li)
    bt_re, bt_im = btr_ref[0], bti_ref[0]
    row = lax.broadcasted_iota(jnp.int32, (LANES, sw), 0)
    lane = lax.broadcasted_iota(jnp.int32, (LANES, sw), 1)
    same = (row // gc) == (lane // ns)
    bb_re = jnp.where(same, f_re * bt_re - f_im * bt_im, 0.0)
    bb_im = jnp.where(same, f_re * bt_im + f_im * bt_re, 0.0)
    for j in range(ch):
        p_re, p_im = pw[ch - 1 - j]
        w_ref[0, j * LANES:(j + 1) * LANES, :] = jnp.concatenate(
            [p_re * bb_re - p_im * bb_im, p_re * bb_im + p_im * bb_re], axis=1).astype(BF16)
    al_re, al_im = pw[ch]
    al_ref[0] = jnp.concatenate([jnp.broadcast_to(al_re, (SUBLANES, sw)),
                                 jnp.broadcast_to(al_im, (SUBLANES, sw))], axis=1)

    lrc, lic = lrc_ref[0], lic_ref[0]
    qw = _cpowers(lrc, lic, jnp.exp(lsc_ref[0]), ch)
    row2 = lax.broadcasted_iota(jnp.int32, (sw, LANES), 0)
    lane2 = lax.broadcasted_iota(jnp.int32, (sw, LANES), 1)
    same2 = (row2 // ns) == (lane2 // gc)
    ct_re = jnp.where(same2, ctr_ref[0], 0.0)
    ct_im = jnp.where(same2, cti_ref[0], 0.0)
    hp = lax.Precision.HIGHEST
    mv_ref[0, :aw, :] = jnp.zeros((aw, aw), BF16)
    for e in range(ch + 1):
        q_re, q_im = qw[e]
        e_re = q_re * ct_re - q_im * ct_im
        e_im = q_re * ct_im + q_im * ct_re
        if e < ch:
            k_e = (jnp.dot(bb_re, e_re, precision=hp, preferred_element_type=F32)
                   - jnp.dot(bb_im, e_im, precision=hp, preferred_element_type=F32)).astype(BF16)
            for j in range(ch - e):
                t = j + e
                mv_ref[0, j * LANES:(j + 1) * LANES, t * LANES:(t + 1) * LANES] = k_e
        if e >= 1:
            t = e - 1
            mv_ref[0, aw:aw + sw, t * LANES:(t + 1) * LANES] = e_re.astype(BF16)
            mv_ref[0, aw + sw:, t * LANES:(t + 1) * LANES] = (-e_im).astype(BF16)


def s5_coef(lam_re, lam_im, log_step, b_re, b_im, c_re, c_im):
    g, p = lam_re.shape
    gc, ch = S5_GROUP, S5_CHUNK
    assert p == S5_STATE and b_re.shape == (g, p, gc) and g % S5_LB == 0
    nb = g // S5_LB
    sw, aw = S5_LB * p, ch * LANES
    row = lambda a: a.reshape(nb, 1, sw)
    col = lambda a: a.reshape(nb, sw, 1)
    ls = jnp.broadcast_to(log_step[:, None], (g, p))

    def bt(b):
        x = b.reshape(nb, S5_LB, p, gc).transpose(0, 3, 1, 2).reshape(nb, 1, gc, sw)
        return jnp.broadcast_to(x, (nb, S5_LB, gc, sw)).reshape(nb, LANES, sw)

    def ct(c):
        x = c.reshape(nb, S5_LB, gc, p).transpose(0, 1, 3, 2).reshape(nb, sw, 1, gc)
        return jnp.broadcast_to(x, (nb, sw, S5_LB, gc)).reshape(nb, sw, LANES)

    vec_r = pl.BlockSpec((1, 1, sw), lambda i: (i, 0, 0))
    vec_c = pl.BlockSpec((1, sw, 1), lambda i: (i, 0, 0))
    return pl.pallas_call(
        _s5_coef_body,
        grid=(nb,),
        in_specs=[vec_r, vec_r, vec_r, vec_c, vec_c, vec_c,
                  pl.BlockSpec((1, LANES, sw), lambda i: (i, 0, 0)),
                  pl.BlockSpec((1, LANES, sw), lambda i: (i, 0, 0)),
                  pl.BlockSpec((1, sw, LANES), lambda i: (i, 0, 0)),
                  pl.BlockSpec((1, sw, LANES), lambda i: (i, 0, 0))],
        out_specs=[pl.BlockSpec((1, aw, 2 * sw), lambda i: (i, 0, 0)),
                   pl.BlockSpec((1, aw + 2 * sw, aw), lambda i: (i, 0, 0)),
                   pl.BlockSpec((1, SUBLANES, 2 * sw), lambda i: (i, 0, 0))],
        out_shape=[jax.ShapeDtypeStruct((nb, aw, 2 * sw), BF16),
                   jax.ShapeDtypeStruct((nb, aw + 2 * sw, aw), BF16),
                   jax.ShapeDtypeStruct((nb, SUBLANES, 2 * sw), F32)],
        compiler_params=_cparams("arbitrary"),
        name="s5_coef",
    )(row(lam_re), row(lam_im), row(ls), col(lam_re), col(lam_im), col(ls),
      bt(b_re), bt(b_im), ct(c_re), ct(c_im))


def _s5_steps(u_ref, chunk0, rows):
    return [u_ref[pl.ds(chunk0 * S5_CHUNK + j, rows, stride=S5_CHUNK), :] for j in range(S5_CHUNK)]


def _s5_emit(z_ref, y, u, d, chunk0, rows):
    for t in range(S5_CHUNK):
        z = jax.nn.gelu(y[:, t * LANES:(t + 1) * LANES] + d * u[t])
        z_ref[pl.ds(chunk0 * S5_CHUNK + t, rows, stride=S5_CHUNK), :] = z


def _s5_prompt_body(u_ref, d_ref, w_ref, mv_ref, al_ref, z_ref, fr_ref, fi_ref, x_ref, h_ref,
                    *, n_chunks, rows):
    sw = x_ref.shape[1] // 2
    a_re = al_ref[0, 0:1, :sw]
    a_im = al_ref[0, 0:1, sw:]

    def tile(k, carry):
        h_re, h_im = carry
        r0 = pl.multiple_of(k * SUBLANES, SUBLANES)
        x = x_ref[pl.ds(r0, SUBLANES), :]
        rows_re, rows_im = [], []
        for s in range(SUBLANES):
            rows_re.append(h_re)
            rows_im.append(h_im)
            n_re = a_re * h_re - a_im * h_im + x[s:s + 1, :sw]
            n_im = a_re * h_im + a_im * h_re + x[s:s + 1, sw:]
            h_re, h_im = n_re, n_im
        h_ref[pl.ds(r0, SUBLANES), :] = jnp.concatenate(
            [jnp.concatenate(rows_re, axis=0), jnp.concatenate(rows_im, axis=0)], axis=1)
        return h_re, h_im

    state = (jnp.zeros((1, sw), F32), jnp.zeros((1, sw), F32))
    for c0 in range(0, n_chunks, rows):
        u = _s5_steps(u_ref, c0, rows)
        a = jnp.concatenate(u, axis=1).astype(BF16)
        x_ref[...] = jnp.dot(a, w_ref[0], preferred_element_type=F32)
        state = lax.fori_loop(0, rows // SUBLANES, tile, state)
        lhs = jnp.concatenate([a, h_ref[...].astype(BF16)], axis=1)
        y = jnp.dot(lhs, mv_ref[0], preferred_element_type=F32)
        _s5_emit(z_ref, y, u, d_ref[...], c0, rows)
    fr_ref[...] = jnp.broadcast_to(state[0], fr_ref.shape)
    fi_ref[...] = jnp.broadcast_to(state[1], fi_ref.shape)


def _s5_sample_body(u_ref, d_ref, w_ref, mv_ref, al_ref, hr_ref, hi_ref, z_ref, fr_ref, fi_ref,
                    *, n_chunks):
    sw = hr_ref.shape[1]
    a_re = al_ref[0, 0:1, :sw]
    a_im = al_ref[0, 0:1, sw:]
    u = _s5_steps(u_ref, 0, n_chunks)
    a = jnp.concatenate(u, axis=1).astype(BF16)
    x = jnp.dot(a, w_ref[0], preferred_element_type=F32)
    h_re, h_im = hr_ref[...], hi_ref[...]
    lhs = jnp.concatenate([a, h_re.astype(BF16), h_im.astype(BF16)], axis=1)
    y = jnp.dot(lhs, mv_ref[0], preferred_element_type=F32)
    _s5_emit(z_ref, y, u, d_ref[...], 0, n_chunks)
    fr_ref[...] = a_re * h_re - a_im * h_im + x[:, :sw]
    fi_ref[...] = a_re * h_im + a_im * h_re + x[:, sw:]


def s5_core(qkvu, coef, d_skip, ssm_width, h0, rows=256):
    w_all, mv_all, al = coef
    nb, aw, sw2 = w_all.shape
    sw = sw2 // 2
    t = qkvu.shape[0]
    n_chunks = t // S5_CHUNK
    assert t % S5_CHUNK == 0 and nb * LANES == ssm_width
    u_blk0 = (qkvu.shape[1] - ssm_width) // LANES
    in_specs = [pl.BlockSpec((t, LANES), lambda b: (0, u_blk0 + b)),
                pl.BlockSpec((1, LANES), lambda b: (0, b)),
                pl.BlockSpec((1, aw, sw2), lambda b: (b, 0, 0)),
                pl.BlockSpec((1, aw + sw2, aw), lambda b: (b, 0, 0)),
                pl.BlockSpec((1, SUBLANES, sw2), lambda b: (b, 0, 0))]
    z_spec = pl.BlockSpec((t, LANES), lambda b: (0, b))
    z_shape = jax.ShapeDtypeStruct((t, ssm_width), F32)
    args = (qkvu, d_skip.reshape(1, ssm_width), w_all, mv_all, al)
    if h0 is None:
        rows = min(rows, n_chunks)
        assert n_chunks % rows == 0 and rows % SUBLANES == 0
        st = pl.BlockSpec((SUBLANES, sw), lambda b: (0, b))
        z, f_re, f_im = pl.pallas_call(
            functools.partial(_s5_prompt_body, n_chunks=n_chunks, rows=rows),
            grid=(nb,),
            in_specs=in_specs,
            out_specs=[z_spec, st, st],
            out_shape=[z_shape] + [jax.ShapeDtypeStruct((SUBLANES, nb * sw), F32)] * 2,
            scratch_shapes=[pltpu.VMEM((rows, sw2), F32), pltpu.VMEM((rows, sw2), F32)],
            compiler_params=_cparams("arbitrary"),
            name="s5_prompt",
        )(*args)
        return z, f_re[:1], f_im[:1]
    assert h0[0].shape == (n_chunks, nb * sw)
    st = pl.BlockSpec((n_chunks, sw), lambda b: (0, b))
    return pl.pallas_call(
        functools.partial(_s5_sample_body, n_chunks=n_chunks),
        grid=(nb,),
        in_specs=in_specs + [st, st],
        out_specs=[z_spec, st, st],
        out_shape=[z_shape] + [jax.ShapeDtypeStruct((n_chunks, nb * sw), F32)] * 2,
        compiler_params=_cparams("arbitrary"),
        name="s5_sample",
    )(*args, *h0)


def _s5_glu_body(z_ref, w_ref, b_ref, o_ref, zb_ref, *, tn):
    j = pl.program_id(1)

    @pl.when(j == 0)
    def _():
        zb_ref[...] = z_ref[...].astype(BF16)

    col = pl.multiple_of(j * tn, tn)
    lin = jnp.dot(zb_ref[...], w_ref[...], preferred_element_type=F32) + b_ref[...]
    o_ref[...] = z_ref[:, pl.ds(col, tn)] * jax.nn.sigmoid(lin)


def s5_glu(z, w_glu, b_glu, tm=512, tn=512):
    t, ws = z.shape
    tm = min(tm, t)
    return pl.pallas_call(
        functools.partial(_s5_glu_body, tn=tn),
        grid=(t // tm, ws // tn),
        in_specs=[pl.BlockSpec((tm, ws), lambda i, j: (i, 0)),
                  pl.BlockSpec((ws, tn), lambda i, j: (0, j)),
                  pl.BlockSpec((1, tn), lambda i, j: (0, j))],
        out_specs=pl.BlockSpec((tm, tn), lambda i, j: (i, j)),
        out_shape=jax.ShapeDtypeStruct((t, ws), F32),
        scratch_shapes=[pltpu.VMEM((tm, ws), BF16)],
        compiler_params=_cparams("arbitrary", "arbitrary"),
        name="s5_glu",
    )(z, w_glu, b_glu.reshape(1, ws))


def _layer(x, mod, attend, h0, lw, coef, dims):
    d_model, attn_w, ssm_w = dims
    h = norm_mod(x, lw["g_norm1"], mod, 1, 0)
    qkvu = inproj(h, lw["w_in"], lw["g_q"], lw["g_k"], attn_w)
    o_attn = attend(qkvu)
    z, f_re, f_im = s5_core(qkvu, coef, lw["s5_d"], ssm_w, h0)
    o_ssm = s5_glu(z, lw["w_glu"], lw["b_glu"])
    mix = rms2(o_attn, o_ssm, lw["g_attn_out"], lw["g_ssm_out"])
    x1 = matmul_res(mix, lw["w_out"], x, mod, 2, tm=1024, tn=512)
    h2 = norm_mod(x1, lw["g_norm2"], mod, 4, 3)
    hid = gateup(h2, lw["w_gate"], lw["w_up"])
    y = matmul_res(hid, lw["w_down"], x1, mod, 5, tm=512, tn=256)
    k = qkvu[:, attn_w:2 * attn_w]
    v = qkvu[:, 2 * attn_w:3 * attn_w]
    return y, k, v, f_re, f_im


def kernel(x_prompt, x_sample, cache_k, cache_v, state_s5_re, state_s5_im, page_table, c_prompt, c_sample, w_ada, b_ada, g_norm1, w_in, g_q, g_k, s5_lam_re, s5_lam_im, s5_log_step, s5_b_re, s5_b_im, s5_c_re, s5_c_im, s5_d, w_glu, b_glu, g_attn_out, g_ssm_out, w_out, g_norm2, w_gate, w_up, w_down):
    depth = w_ada.shape[0]
    batch, seq, d_model = x_prompt.shape
    dec_batch, dec_seq, _ = x_sample.shape
    n_pool, n_heads, head_dim = cache_k.shape[1], cache_k.shape[3], cache_k.shape[4]
    assert head_dim == HEAD_DIM and cache_k.shape[2] == PAGE_SIZE and batch == 1
    assert dec_seq == S5_CHUNK
    attn_w = n_heads * head_dim
    ssm_w = d_model - attn_w
    n_groups, n_state = state_s5_re.shape[2], state_s5_re.shape[3]
    past_len = page_table.shape[1] * PAGE_SIZE
    dims = (d_model, attn_w, ssm_w)
    ck = cache_k.reshape(depth, n_pool, PAGE_SIZE * n_heads, head_dim)
    cv = cache_v.reshape(depth, n_pool, PAGE_SIZE * n_heads, head_dim)

    y_p = x_prompt.reshape(batch * seq, d_model)
    y_s = x_sample.reshape(dec_batch * dec_seq, d_model)
    outs = [[] for _ in range(8)]
    for l in range(depth):
        lw = dict(g_norm1=g_norm1[l], w_in=w_in[l].astype(BF16), g_q=g_q[l], g_k=g_k[l],
                  s5_d=s5_d[l], w_glu=w_glu[l].astype(BF16), b_glu=b_glu[l],
                  g_attn_out=g_attn_out[l], g_ssm_out=g_ssm_out[l], w_out=w_out[l].astype(BF16),
                  g_norm2=g_norm2[l], w_gate=w_gate[l].astype(BF16), w_up=w_up[l].astype(BF16),
                  w_down=w_down[l].astype(BF16))
        coef = s5_coef(s5_lam_re[l], s5_lam_im[l], s5_log_step[l], s5_b_re[l], s5_b_im[l],
                       s5_c_re[l], s5_c_im[l])
        pad = (-(dec_batch + batch)) % SUBLANES
        c_all = jnp.concatenate([c_sample, c_prompt, jnp.zeros((pad, d_model), F32)], axis=0)
        mod = adaln(c_all, w_ada[l], b_ada[l])
        mod_p = mod[dec_batch:dec_batch + 1]
        mod_s = jnp.repeat(mod[:dec_batch], dec_seq, axis=0)
        h0 = (state_s5_re[l].reshape(dec_batch, n_groups * n_state),
              state_s5_im[l].reshape(dec_batch, n_groups * n_state))

        y_p, kp, vp, hrp, hip = _layer(
            y_p, mod_p, functools.partial(moba_prompt, n_heads=n_heads), None, lw, coef, dims)
        y_s, ks, vs, hrs, his = _layer(
            y_s, mod_s,
            functools.partial(moba_sample, cache_k=ck, cache_v=cv, layer=l, page_table=page_table,
                              n_heads=n_heads, t_new=dec_seq, past_len=past_len),
            h0, lw, coef, dims)
        vals = (kp.reshape(batch, seq, n_heads, head_dim), vp.reshape(batch, seq, n_heads, head_dim),
                hrp.reshape(batch, n_groups, n_state), hip.reshape(batch, n_groups, n_state),
                ks.reshape(dec_batch, dec_seq, n_heads, head_dim),
                vs.reshape(dec_batch, dec_seq, n_heads, head_dim),
                hrs.reshape(dec_batch, n_groups, n_state), his.reshape(dec_batch, n_groups, n_state))
        for o, val in zip(outs, vals):
            o.append(val)
    return (y_p.reshape(batch, seq, d_model), y_s.reshape(dec_batch, dec_seq, d_model),
            *[jnp.stack(o) for o in outs])
```

```python
import functools

import jax
import jax.numpy as jnp
from jax import lax
from jax.experimental import pallas as pl
from jax.experimental.pallas import tpu as pltpu

F32 = jnp.float32
BF16 = jnp.bfloat16

HEAD_DIM = 128
MOBA_BLOCK = 256
MOBA_TOPK = 3
PAST_UNROLL = 4
PROMPT_HEADS = 2
SAMPLE_BLOCKS = 2
PAGE_SIZE = 128
S5_GROUP = 16
S5_STATE = 64
S5_CHUNK = 8
EPS = 1e-6
NEG = -1e30
LOG2E = 1.4426950408889634

LANES = 128
SUBLANES = 8
S5_LB = LANES // S5_GROUP
VMEM_LIMIT = 56 * 1024 * 1024


def _cparams(*sem):
    return pltpu.CompilerParams(dimension_semantics=sem, vmem_limit_bytes=VMEM_LIMIT)


def _adaln_body(c_ref, w_ref, b_ref, o_ref):
    c = c_ref[...]
    s = (c * jax.nn.sigmoid(c)).astype(BF16)
    o_ref[...] = jnp.dot(s, w_ref[...].astype(BF16), preferred_element_type=F32) + b_ref[...]


def adaln(c, w_ada, b_ada, tn=512):
    m, d = c.shape
    n = w_ada.shape[1]
    return pl.pallas_call(
        _adaln_body,
        grid=(n // tn,),
        in_specs=[pl.BlockSpec((m, d), lambda j: (0, 0)),
                  pl.BlockSpec((d, tn), lambda j: (0, j)),
                  pl.BlockSpec((1, tn), lambda j: (0, j))],
        out_specs=pl.BlockSpec((m, tn), lambda j: (0, j)),
        out_shape=jax.ShapeDtypeStruct((m, n), F32),
        compiler_params=_cparams("arbitrary"),
        name="adaln",
    )(c, w_ada, b_ada.reshape(1, n))


def _mod_spec(mod, tm, tn, chunk, n_col_blocks):
    if mod.shape[0] == 1:
        return pl.BlockSpec((1, tn), lambda i, j: (0, chunk * n_col_blocks + j))
    return pl.BlockSpec((tm, tn), lambda i, j: (i, chunk * n_col_blocks + j))


def _norm_mod_body(x_ref, g_ref, sc_ref, sh_ref, o_ref):
    x = x_ref[...]
    r = lax.rsqrt(jnp.mean(x * x, axis=-1, keepdims=True) + EPS)
    h = ((x * r) * g_ref[...]) * (1.0 + sc_ref[...]) + sh_ref[...]
    o_ref[...] = h.astype(BF16)


def norm_mod(x, g, mod, sc_chunk, sh_chunk, tm=256):
    t, d = x.shape
    tm = min(tm, t)
    return pl.pallas_call(
        _norm_mod_body,
        grid=(t // tm, 1),
        in_specs=[pl.BlockSpec((tm, d), lambda i, j: (i, 0)),
                  pl.BlockSpec((1, d), lambda i, j: (0, 0)),
                  _mod_spec(mod, tm, d, sc_chunk, 1),
                  _mod_spec(mod, tm, d, sh_chunk, 1)],
        out_specs=pl.BlockSpec((tm, d), lambda i, j: (i, 0)),
        out_shape=jax.ShapeDtypeStruct((t, d), BF16),
        compiler_params=_cparams("arbitrary", "arbitrary"),
        name="norm_mod",
    )(x, g.reshape(1, d), mod, mod)


def _inproj_body(h_ref, w_ref, gq_ref, gk_ref, o_ref, wb_ref, *, tn, attn_width):
    j = pl.program_id(0)

    @pl.when(pl.program_id(1) == 0)
    def _():
        wb_ref[...] = w_ref[...].astype(BF16)

    acc = jnp.dot(h_ref[...], wb_ref[...], preferred_element_type=F32)
    col0 = j * tn

    @pl.when(col0 < 2 * attn_width)
    def _():
        g = jnp.where(col0 < attn_width, gq_ref[...], gk_ref[...])
        for s in range(tn // HEAD_DIM):
            a = acc[:, s * HEAD_DIM:(s + 1) * HEAD_DIM]
            r = lax.rsqrt(jnp.mean(a * a, axis=-1, keepdims=True) + EPS)
            o_ref[:, s * HEAD_DIM:(s + 1) * HEAD_DIM] = (a * r) * g

    @pl.when(col0 >= 2 * attn_width)
    def _():
        o_ref[...] = acc


def inproj(h, w_in, layer, g_q, g_k, attn_width, tm=1024, tn=512):
    t, d = h.shape
    n = w_in.shape[2]
    tm = min(tm, t)
    assert attn_width % tn == 0
    return pl.pallas_call(
        functools.partial(_inproj_body, tn=tn, attn_width=attn_width),
        grid=(n // tn, t // tm),
        in_specs=[pl.BlockSpec((tm, d), lambda j, i: (i, 0)),
                  pl.BlockSpec((None, d, tn), lambda j, i: (layer, 0, j)),
                  pl.BlockSpec((1, HEAD_DIM), lambda j, i: (0, 0)),
                  pl.BlockSpec((1, HEAD_DIM), lambda j, i: (0, 0))],
        out_specs=pl.BlockSpec((tm, tn), lambda j, i: (i, j)),
        out_shape=jax.ShapeDtypeStruct((t, n), F32),
        scratch_shapes=[pltpu.VMEM((d, tn), BF16)],
        compiler_params=_cparams("arbitrary", "arbitrary"),
        name="inproj",
    )(h, w_in, g_q.reshape(1, HEAD_DIM), g_k.reshape(1, HEAD_DIM))


def _matmul_res_body(a_ref, w_ref, res_ref, gt_ref, o_ref):
    acc = jnp.dot(a_ref[...], w_ref[...], preferred_element_type=F32)
    o_ref[...] = res_ref[...] + gt_ref[...] * acc


def matmul_res(a, w, res, mod, gt_chunk, tm, tn):
    t, k = a.shape
    n = w.shape[1]
    tm = min(tm, t)
    return pl.pallas_call(
        _matmul_res_body,
        grid=(t // tm, n // tn),
        in_specs=[pl.BlockSpec((tm, k), lambda i, j: (i, 0)),
                  pl.BlockSpec((k, tn), lambda i, j: (0, j)),
                  pl.BlockSpec((tm, tn), lambda i, j: (i, j)),
                  _mod_spec(mod, tm, tn, gt_chunk, n // tn)],
        out_specs=pl.BlockSpec((tm, tn), lambda i, j: (i, j)),
        out_shape=jax.ShapeDtypeStruct((t, n), F32),
        compiler_params=_cparams("arbitrary", "arbitrary"),
        name="matmul_res",
    )(a, w, res, mod)


def _gateup_body(h_ref, wg_ref, wu_ref, o_ref, wgb_ref, wub_ref):
    @pl.when(pl.program_id(1) == 0)
    def _():
        wgb_ref[...] = wg_ref[...].astype(BF16)
        wub_ref[...] = wu_ref[...].astype(BF16)

    h = h_ref[...]
    g = jnp.dot(h, wgb_ref[...], preferred_element_type=F32)
    u = jnp.dot(h, wub_ref[...], preferred_element_type=F32)
    o_ref[...] = ((g * jax.nn.sigmoid(g)) * u).astype(BF16)


def gateup(h, w_gate, w_up, layer, tm=1024, tn=256):
    t, d = h.shape
    n = w_gate.shape[2]
    tm = min(tm, t)
    return pl.pallas_call(
        _gateup_body,
        grid=(n // tn, t // tm),
        in_specs=[pl.BlockSpec((tm, d), lambda j, i: (i, 0)),
                  pl.BlockSpec((None, d, tn), lambda j, i: (layer, 0, j)),
                  pl.BlockSpec((None, d, tn), lambda j, i: (layer, 0, j))],
        out_specs=pl.BlockSpec((tm, tn), lambda j, i: (i, j)),
        out_shape=jax.ShapeDtypeStruct((t, n), BF16),
        scratch_shapes=[pltpu.VMEM((d, tn), BF16), pltpu.VMEM((d, tn), BF16)],
        compiler_params=_cparams("arbitrary", "arbitrary"),
        name="gateup",
    )(h, w_gate, w_up)


def _rms2_body(a_ref, s_ref, ga_ref, gs_ref, o_ref, *, wa):
    a = a_ref[...]
    ra = lax.rsqrt(jnp.mean(a * a, axis=-1, keepdims=True) + EPS)
    o_ref[:, :wa] = ((a * ra) * ga_ref[...]).astype(BF16)
    s = s_ref[...]
    rs = lax.rsqrt(jnp.mean(s * s, axis=-1, keepdims=True) + EPS)
    o_ref[:, wa:] = ((s * rs) * gs_ref[...]).astype(BF16)


def rms2(o_attn, o_ssm, g_a, g_s, tm=256):
    t, wa = o_attn.shape
    ws = o_ssm.shape[1]
    tm = min(tm, t)
    return pl.pallas_call(
        functools.partial(_rms2_body, wa=wa),
        grid=(t // tm,),
        in_specs=[pl.BlockSpec((tm, wa), lambda i: (i, 0)),
                  pl.BlockSpec((tm, ws), lambda i: (i, 0)),
                  pl.BlockSpec((1, wa), lambda i: (0, 0)),
                  pl.BlockSpec((1, ws), lambda i: (0, 0))],
        out_specs=pl.BlockSpec((tm, wa + ws), lambda i: (i, 0)),
        out_shape=jax.ShapeDtypeStruct((t, wa + ws), BF16),
        compiler_params=_cparams("arbitrary"),
        name="rms2",
    )(o_attn, o_ssm, g_a.reshape(1, wa), g_s.reshape(1, ws))


def _top_mask_rows(s, n_avail, n_sel):
    row = lax.broadcasted_iota(jnp.int32, s.shape, 0)
    s = jnp.where(row < n_avail, s, -jnp.inf)
    sel = jnp.zeros(s.shape, F32)
    for _ in range(n_sel):
        mx = jnp.max(s, axis=0, keepdims=True)
        first = jnp.min(jnp.where(s == mx, row, s.shape[0]), axis=0, keepdims=True)
        hit = row == first
        sel = jnp.where(hit, 1.0, sel)
        s = jnp.where(hit, -jnp.inf, s)
    return jnp.where(row < n_avail, sel, 0.0)


def _moba_prompt_body(q_ref, k_ref, v_ref, o_ref, kb_ref, vt_ref, km_ref, sel_ref, *, n_blk, n_sel):
    i = pl.program_id(1)
    blk = MOBA_BLOCK
    heads = range(PROMPT_HEADS)
    cols = [slice(x * HEAD_DIM, (x + 1) * HEAD_DIM) for x in heads]

    @pl.when(i == 0)
    def _():
        km_ref[...] = jnp.zeros(km_ref.shape, F32)
        for x in heads:
            kb_ref[x] = k_ref[:, cols[x]].astype(BF16)
            for b in range(n_blk):
                rows = slice(b * blk, (b + 1) * blk)
                vt_ref[x, :, rows] = v_ref[rows, cols[x]].T.astype(BF16)
                km_ref[x, b:b + 1, :] = jnp.mean(k_ref[rows, cols[x]], axis=0, keepdims=True)

    q_t = [q_ref[:, cols[x]].T for x in heads]
    qs_t = [(q_t[x] * (HEAD_DIM ** -0.5 * LOG2E)).astype(BF16) for x in heads]

    row0 = pl.multiple_of(i * blk, blk)
    k_id = lax.broadcasted_iota(jnp.int32, (blk, blk), 0)
    q_id = lax.broadcasted_iota(jnp.int32, (blk, blk), 1)
    state = []
    for x in heads:
        s = jnp.dot(kb_ref[x, pl.ds(row0, blk), :], qs_t[x], preferred_element_type=F32)
        s = jnp.where(k_id <= q_id, s, NEG)
        m = jnp.max(s, axis=0, keepdims=True)
        p = jnp.exp2(s - m)
        l = jnp.sum(p, axis=0, keepdims=True)
        acc = jnp.dot(vt_ref[x, :, pl.ds(row0, blk)], p.astype(BF16), preferred_element_type=F32)
        state += [m, l, acc]

    if n_sel > 0:
        for x in heads:
            sc = jnp.dot(km_ref[x], q_t[x], precision=lax.Precision.HIGHEST, preferred_element_type=F32)
            sel_ref[x] = _top_mask_rows(sc, i, n_sel)

        def group(g, carry):
            carry = list(carry)
            js = [jnp.minimum(g * PAST_UNROLL + u, n_blk - 1) for u in range(PAST_UNROLL)]
            r0s = [pl.multiple_of(j * blk, blk) for j in js]
            for x in heads:
                m, l, acc = carry[3 * x:3 * x + 3]
                ss = [jnp.dot(kb_ref[x, pl.ds(r0, blk), :], qs_t[x], preferred_element_type=F32)
                      for r0 in r0s]
                for j, r0, s in zip(js, r0s, ss):
                    on = sel_ref[x, pl.ds(j, 1), :]
                    m_c = jnp.maximum(m, jnp.max(s, axis=0, keepdims=True))
                    p = jnp.exp2(s - m_c)
                    m_n = jnp.where(on > 0.0, m_c, m)
                    alpha = jnp.exp2(m - m_n)
                    l = alpha * l + on * jnp.sum(p, axis=0, keepdims=True)
                    pv = jnp.dot(vt_ref[x, :, pl.ds(r0, blk)], p.astype(BF16), preferred_element_type=F32)
                    acc = alpha * acc + on * pv
                    m = m_n
                carry[3 * x:3 * x + 3] = [m, l, acc]
            return tuple(carry)

        n_trips = (i + PAST_UNROLL - 1) // PAST_UNROLL
        state = lax.fori_loop(0, n_trips, group, tuple(state))

    for x in heads:
        m, l, acc = state[3 * x:3 * x + 3]
        o_ref[:, cols[x]] = (acc / l).T


def moba_prompt(qkvu, n_heads):
    s_len = qkvu.shape[0]
    blk = MOBA_BLOCK
    n_blk = s_len // blk
    n_cand = (s_len - 1) // blk
    n_sel = min(MOBA_TOPK, n_cand)
    hp = PROMPT_HEADS
    assert s_len % blk == 0 and n_heads % hp == 0
    n_hg = n_heads // hp
    n_cand_rows = -(-n_blk // SUBLANES) * SUBLANES
    return pl.pallas_call(
        functools.partial(_moba_prompt_body, n_blk=n_blk, n_sel=n_sel),
        grid=(n_hg, n_blk),
        in_specs=[pl.BlockSpec((blk, hp * HEAD_DIM), lambda h, i: (i, h)),
                  pl.BlockSpec((s_len, hp * HEAD_DIM), lambda h, i: (0, n_hg + h)),
                  pl.BlockSpec((s_len, hp * HEAD_DIM), lambda h, i: (0, 2 * n_hg + h))],
        out_specs=pl.BlockSpec((blk, hp * HEAD_DIM), lambda h, i: (i, h)),
        out_shape=jax.ShapeDtypeStruct((s_len, n_heads * HEAD_DIM), F32),
        scratch_shapes=[pltpu.VMEM((hp, s_len, HEAD_DIM), BF16),
                        pltpu.VMEM((hp, HEAD_DIM, s_len), BF16),
                        pltpu.VMEM((hp, n_cand_rows, HEAD_DIM), F32),
                        pltpu.VMEM((hp, n_cand_rows, blk), F32)],
        compiler_params=_cparams("arbitrary", "arbitrary"),
        name="moba_prompt",
    )(qkvu, qkvu, qkvu)


def _head_sums(x, n_heads):
    parts = []
    for h in range(n_heads):
        sm = jnp.sum(x[:, h * HEAD_DIM:(h + 1) * HEAD_DIM], axis=-1, keepdims=True)
        parts.append(jnp.broadcast_to(sm, (x.shape[0], HEAD_DIM)))
    return jnp.concatenate(parts, axis=1)


def _expand_stat(col, n_heads, t):
    half = n_heads // 2
    parts = []
    for h in range(n_heads):
        r0 = ((h % half) * 2 + h // half) * t
        parts.append(jnp.broadcast_to(col[r0:r0 + t, :], (t, HEAD_DIM)))
    return jnp.concatenate(parts, axis=1)


def _moba_sample_body(pt_ref, q_ref, kn_ref, vn_ref, *refs, n_heads, n_blk, n_sel, t_new):
    del pt_ref
    nb = SAMPLE_BLOCKS
    k_pages, v_pages = refs[:2 * nb], refs[2 * nb:4 * nb]
    o_ref, m_ref, l_ref, acc_ref, ks_ref = refs[4 * nb:]
    step = pl.program_id(1)
    for x in range(nb):
        _moba_sample_block(q_ref, k_pages[2 * x], k_pages[2 * x + 1], v_pages[2 * x], v_pages[2 * x + 1],
                           m_ref, l_ref, acc_ref, ks_ref, step * nb + x, n_heads=n_heads, t_new=t_new)

    @pl.when(step == n_blk // nb - 1)
    def _():
        _moba_sample_merge(q_ref, kn_ref, vn_ref, o_ref, m_ref, l_ref, acc_ref, ks_ref,
                           n_heads=n_heads, n_blk=n_blk, n_sel=n_sel, t_new=t_new)


def _moba_sample_block(q_ref, ka_ref, kb_ref, va_ref, vb_ref, m_ref, l_ref, acc_ref, ks_ref, b,
                       *, n_heads, t_new):
    w = n_heads * HEAD_DIM
    half = n_heads // 2
    nt = (((1,), (1,)), ((), ()))

    q = q_ref[...]
    qs = q * (HEAD_DIM ** -0.5)

    def pair_rows(pa_ref, pb_ref, c):
        return jnp.concatenate([pa_ref[pl.ds(c, 2 * PAGE_SIZE, stride=half), :],
                                pb_ref[pl.ds(c, 2 * PAGE_SIZE, stride=half), :]], axis=0)

    n_key = 2 * MOBA_BLOCK
    row_head = lax.broadcasted_iota(jnp.int32, (2 * t_new, n_key), 0) // t_new
    key_head = lax.broadcasted_iota(jnp.int32, (2 * t_new, n_key), 1) % 2
    own = row_head == key_head
    sub_odd = lax.broadcasted_iota(jnp.int32, (SUBLANES, HEAD_DIM), 0) % 2 == 1
    ksum = [None] * n_heads
    s_parts = []
    for c in range(half):
        kp = pair_rows(ka_ref, kb_ref, c)
        tile = jnp.sum(kp.reshape(n_key // SUBLANES, SUBLANES, HEAD_DIM), axis=0)
        ksum[c] = jnp.sum(jnp.where(sub_odd, 0.0, tile), axis=0, keepdims=True)
        ksum[c + half] = jnp.sum(jnp.where(sub_odd, tile, 0.0), axis=0, keepdims=True)
        q2 = jnp.concatenate([qs[:, c * HEAD_DIM:(c + 1) * HEAD_DIM],
                              qs[:, (c + half) * HEAD_DIM:(c + half + 1) * HEAD_DIM]], axis=0)
        sc2 = lax.dot_general(q2.astype(BF16), kp.astype(BF16), nt, preferred_element_type=F32)
        s_parts.append(jnp.where(own, sc2, NEG))
    ks_ref[b] = jnp.broadcast_to(jnp.concatenate(ksum, axis=1), (t_new, w))
    s = jnp.concatenate(s_parts, axis=0)
    m = jnp.max(s, axis=-1, keepdims=True)
    p = jnp.exp(s - m)
    l = jnp.sum(p, axis=-1, keepdims=True)
    pb = p.astype(BF16)
    o_parts = [None] * n_heads
    for c in range(half):
        vp = pair_rows(va_ref, vb_ref, c).astype(BF16)
        r = jnp.dot(pb[c * 2 * t_new:(c + 1) * 2 * t_new, :], vp, preferred_element_type=F32)
        o_parts[c] = r[:t_new]
        o_parts[c + half] = r[t_new:]
    acc_ref[b] = jnp.concatenate(o_parts, axis=1)
    m_ref[b] = _expand_stat(m, n_heads, t_new)
    l_ref[b] = _expand_stat(l, n_heads, t_new)


def _moba_sample_merge(q_ref, kn_ref, vn_ref, o_ref, m_ref, l_ref, acc_ref, ks_ref,
                       *, n_heads, n_blk, n_sel, t_new):
    w = n_heads * HEAD_DIM
    q = q_ref[...]
    qs = q * (HEAD_DIM ** -0.5)
    sc = [_head_sums(q * (ks_ref[j] * (1.0 / MOBA_BLOCK)), n_heads) for j in range(n_blk)]
    sel = [jnp.zeros((t_new, w), F32) for _ in range(n_blk)]
    for _ in range(n_sel):
        mx = sc[0]
        for j in range(1, n_blk):
            mx = jnp.maximum(mx, sc[j])
        taken = jnp.zeros((t_new, w), F32)
        for j in range(n_blk):
            hit = jnp.logical_and(sc[j] == mx, taken == 0.0)
            sel[j] = jnp.where(hit, 1.0, sel[j])
            sc[j] = jnp.where(hit, -jnp.inf, sc[j])
            taken = jnp.where(hit, 1.0, taken)

    kn = kn_ref[...]
    vn = vn_ref[...]
    t_id = lax.broadcasted_iota(jnp.int32, (t_new, w), 0)
    s_own = []
    for tk in range(t_new):
        so = _head_sums(qs * kn[tk:tk + 1, :], n_heads)
        s_own.append(jnp.where(t_id >= tk, so, NEG))
    m_all = s_own[0]
    for tk in range(1, t_new):
        m_all = jnp.maximum(m_all, s_own[tk])
    for j in range(n_blk):
        m_all = jnp.maximum(m_all, jnp.where(sel[j] > 0.0, m_ref[j], NEG))
    num = jnp.zeros((t_new, w), F32)
    den = jnp.zeros((t_new, w), F32)
    for j in range(n_blk):
        wj = sel[j] * jnp.exp(jnp.where(sel[j] > 0.0, m_ref[j], NEG) - m_all)
        num = num + wj * acc_ref[j]
        den = den + wj * l_ref[j]
    for tk in range(t_new):
        pk = jnp.exp(s_own[tk] - m_all)
        num = num + pk * vn[tk:tk + 1, :]
        den = den + pk
    o_ref[...] = num / den


def moba_sample(qkvu, cache_k, cache_v, layer, page_table, n_heads, t_new, past_len):
    n_seq = page_table.shape[0]
    w = n_heads * HEAD_DIM
    ppb = MOBA_BLOCK // PAGE_SIZE
    n_blk = past_len // MOBA_BLOCK
    n_sel = min(MOBA_TOPK, n_blk)
    nb = SAMPLE_BLOCKS
    assert past_len % MOBA_BLOCK == 0 and n_blk % nb == 0 and ppb == 2 and t_new <= MOBA_BLOCK
    n_pages = page_table.shape[1]
    pt = page_table.reshape(-1)

    def page_spec(page):
        return pl.BlockSpec((None, None, PAGE_SIZE * n_heads, HEAD_DIM),
                            lambda s, b, pt: (layer, pt[s * n_pages + b * nb * ppb + page], 0, 0))

    page_specs = [page_spec(page) for page in range(nb * ppb)]
    return pl.pallas_call(
        functools.partial(_moba_sample_body, n_heads=n_heads, n_blk=n_blk, n_sel=n_sel, t_new=t_new),
        grid_spec=pltpu.PrefetchScalarGridSpec(
            num_scalar_prefetch=1,
            grid=(n_seq, n_blk // nb),
            in_specs=[pl.BlockSpec((t_new, w), lambda s, b, pt: (s, 0)),
                      pl.BlockSpec((t_new, w), lambda s, b, pt: (s, 1)),
                      pl.BlockSpec((t_new, w), lambda s, b, pt: (s, 2))] + page_specs + page_specs,
            out_specs=pl.BlockSpec((t_new, w), lambda s, b, pt: (s, 0)),
            scratch_shapes=[pltpu.VMEM((n_blk, t_new, w), F32),
                            pltpu.VMEM((n_blk, t_new, w), F32),
                            pltpu.VMEM((n_blk, t_new, w), F32),
                            pltpu.VMEM((n_blk, t_new, w), F32)]),
        out_shape=jax.ShapeDtypeStruct((n_seq * t_new, w), F32),
        compiler_params=_cparams("arbitrary", "arbitrary"),
        name="moba_sample",
    )(pt, qkvu, qkvu, qkvu, *([cache_k] * (nb * ppb)), *([cache_v] * (nb * ppb)))


def _cpowers(lr, li, step, n):
    mag = jnp.exp(lr * step)
    a_re, a_im = mag * jnp.cos(li * step), mag * jnp.sin(li * step)
    out = [(jnp.ones_like(a_re), jnp.zeros_like(a_im))]
    for _ in range(n):
        p_re, p_im = out[-1]
        out.append((p_re * a_re - p_im * a_im, p_re * a_im + p_im * a_re))
    return out


def _zoh_gain(a_re, a_im, lr, li):
    den = lr * lr + li * li
    f_re = ((a_re - 1.0) * lr + a_im * li) / den
    f_im = (a_im * lr - (a_re - 1.0) * li) / den
    return f_re, f_im


def _s5_coef_body(lrr_ref, lir_ref, lsr_ref, lrc_ref, lic_ref, lsc_ref,
                  btr_ref, bti_ref, ctr_ref, cti_ref, w_ref, mv_ref, al_ref):
    gc, ch, ns = S5_GROUP, S5_CHUNK, S5_STATE
    sw = S5_LB * ns
    aw = ch * LANES
    lr, li = lrr_ref[0], lir_ref[0]
    pw = _cpowers(lr, li, jnp.exp(lsr_ref[0]), ch)
    f_re, f_im = _zoh_gain(pw[1][0], pw[1][1], lr, li)
    bt_re, bt_im = btr_ref[0], bti_ref[0]
    row = lax.broadcasted_iota(jnp.int32, (LANES, sw), 0)
    lane = lax.broadcasted_iota(jnp.int32, (LANES, sw), 1)
    same = (row // gc) == (lane // ns)
    bb_re = jnp.where(same, f_re * bt_re - f_im * bt_im, 0.0)
    bb_im = jnp.where(same, f_re * bt_im + f_im * bt_re, 0.0)
    for j in range(ch):
        p_re, p_im = pw[ch - 1 - j]
        w_ref[0, j * LANES:(j + 1) * LANES, :] = jnp.concatenate(
            [p_re * bb_re - p_im * bb_im, p_re * bb_im + p_im * bb_re], axis=1).astype(BF16)
    al_re, al_im = pw[ch]
    al_ref[0] = jnp.concatenate([jnp.broadcast_to(al_re, (SUBLANES, sw)),
                                 jnp.broadcast_to(al_im, (SUBLANES, sw))], axis=1)

    lrc, lic = lrc_ref[0], lic_ref[0]
    qw = _cpowers(lrc, lic, jnp.exp(lsc_ref[0]), ch)
    row2 = lax.broadcasted_iota(jnp.int32, (sw, LANES), 0)
    lane2 = lax.broadcasted_iota(jnp.int32, (sw, LANES), 1)
    same2 = (row2 // ns) == (lane2 // gc)
    ct_re = jnp.where(same2, ctr_ref[0], 0.0)
    ct_im = jnp.where(same2, cti_ref[0], 0.0)
    hp = lax.Precision.HIGHEST
    mv_ref[0, :aw, :] = jnp.zeros((aw, aw), BF16)
    for e in range(ch + 1):
        q_re, q_im = qw[e]
        e_re = q_re * ct_re - q_im * ct_im
        e_im = q_re * ct_im + q_im * ct_re
        if e < ch:
            k_e = (jnp.dot(bb_re, e_re, precision=hp, preferred_element_type=F32)
                   - jnp.dot(bb_im, e_im, precision=hp, preferred_element_type=F32)).astype(BF16)
            for j in range(ch - e):
                t = j + e
                mv_ref[0, j * LANES:(j + 1) * LANES, t * LANES:(t + 1) * LANES] = k_e
        if e >= 1:
            t = e - 1
            mv_ref[0, aw:aw + sw, t * LANES:(t + 1) * LANES] = e_re.astype(BF16)
            mv_ref[0, aw + sw:, t * LANES:(t + 1) * LANES] = (-e_im).astype(BF16)


def s5_coef(lam_re, lam_im, log_step, b_re, b_im, c_re, c_im):
    g, p = lam_re.shape
    gc, ch = S5_GROUP, S5_CHUNK
    assert p == S5_STATE and b_re.shape == (g, p, gc) and g % S5_LB == 0
    nb = g // S5_LB
    sw, aw = S5_LB * p, ch * LANES
    row = lambda a: a.reshape(nb, 1, sw)
    col = lambda a: a.reshape(nb, sw, 1)
    ls = jnp.broadcast_to(log_step[:, None], (g, p))

    def bt(b):
        x = b.reshape(nb, S5_LB, p, gc).transpose(0, 3, 1, 2).reshape(nb, 1, gc, sw)
        return jnp.broadcast_to(x, (nb, S5_LB, gc, sw)).reshape(nb, LANES, sw)

    def ct(c):
        x = c.reshape(nb, S5_LB, gc, p).transpose(0, 1, 3, 2).reshape(nb, sw, 1, gc)
        return jnp.broadcast_to(x, (nb, sw, S5_LB, gc)).reshape(nb, sw, LANES)

    vec_r = pl.BlockSpec((1, 1, sw), lambda i: (i, 0, 0))
    vec_c = pl.BlockSpec((1, sw, 1), lambda i: (i, 0, 0))
    return pl.pallas_call(
        _s5_coef_body,
        grid=(nb,),
        in_specs=[vec_r, vec_r, vec_r, vec_c, vec_c, vec_c,
                  pl.BlockSpec((1, LANES, sw), lambda i: (i, 0, 0)),
                  pl.BlockSpec((1, LANES, sw), lambda i: (i, 0, 0)),
                  pl.BlockSpec((1, sw, LANES), lambda i: (i, 0, 0)),
                  pl.BlockSpec((1, sw, LANES), lambda i: (i, 0, 0))],
        out_specs=[pl.BlockSpec((1, aw, 2 * sw), lambda i: (i, 0, 0)),
                   pl.BlockSpec((1, aw + 2 * sw, aw), lambda i: (i, 0, 0)),
                   pl.BlockSpec((1, SUBLANES, 2 * sw), lambda i: (i, 0, 0))],
        out_shape=[jax.ShapeDtypeStruct((nb, aw, 2 * sw), BF16),
                   jax.ShapeDtypeStruct((nb, aw + 2 * sw, aw), BF16),
                   jax.ShapeDtypeStruct((nb, SUBLANES, 2 * sw), F32)],
        compiler_params=_cparams("arbitrary"),
        name="s5_coef",
    )(row(lam_re), row(lam_im), row(ls), col(lam_re), col(lam_im), col(ls),
      bt(b_re), bt(b_im), ct(c_re), ct(c_im))


def _s5_steps(u_ref, chunk0, rows):
    return [u_ref[pl.ds(chunk0 * S5_CHUNK + j, rows, stride=S5_CHUNK), :] for j in range(S5_CHUNK)]


def _s5_emit(z_ref, y, u, d, chunk0, rows):
    for t in range(S5_CHUNK):
        z = jax.nn.gelu(y[:, t * LANES:(t + 1) * LANES] + d * u[t])
        z_ref[pl.ds(chunk0 * S5_CHUNK + t, rows, stride=S5_CHUNK), :] = z


def _s5_prompt_body(u_ref, d_ref, w_ref, mv_ref, al_ref, z_ref, fr_ref, fi_ref, x_ref, h_ref,
                    *, n_chunks, rows):
    sw = x_ref.shape[1] // 2
    a_re = al_ref[0, 0:1, :sw]
    a_im = al_ref[0, 0:1, sw:]

    def tile(k, carry):
        h_re, h_im = carry
        r0 = pl.multiple_of(k * SUBLANES, SUBLANES)
        x = x_ref[pl.ds(r0, SUBLANES), :]
        rows_re, rows_im = [], []
        for s in range(SUBLANES):
            rows_re.append(h_re)
            rows_im.append(h_im)
            n_re = a_re * h_re - a_im * h_im + x[s:s + 1, :sw]
            n_im = a_re * h_im + a_im * h_re + x[s:s + 1, sw:]
            h_re, h_im = n_re, n_im
        h_ref[pl.ds(r0, SUBLANES), :] = jnp.concatenate(
            [jnp.concatenate(rows_re, axis=0), jnp.concatenate(rows_im, axis=0)], axis=1)
        return h_re, h_im

    state = (jnp.zeros((1, sw), F32), jnp.zeros((1, sw), F32))
    for c0 in range(0, n_chunks, rows):
        u = _s5_steps(u_ref, c0, rows)
        a = jnp.concatenate(u, axis=1).astype(BF16)
        x_ref[...] = jnp.dot(a, w_ref[0], preferred_element_type=F32)
        state = lax.fori_loop(0, rows // SUBLANES, tile, state)
        lhs = jnp.concatenate([a, h_ref[...].astype(BF16)], axis=1)
        y = jnp.dot(lhs, mv_ref[0], preferred_element_type=F32)
        _s5_emit(z_ref, y, u, d_ref[...], c0, rows)
    fr_ref[...] = jnp.broadcast_to(state[0], fr_ref.shape)
    fi_ref[...] = jnp.broadcast_to(state[1], fi_ref.shape)


def _s5_sample_body(u_ref, d_ref, w_ref, mv_ref, al_ref, hr_ref, hi_ref, z_ref, fr_ref, fi_ref,
                    *, n_chunks):
    sw = hr_ref.shape[1]
    a_re = al_ref[0, 0:1, :sw]
    a_im = al_ref[0, 0:1, sw:]
    u = _s5_steps(u_ref, 0, n_chunks)
    a = jnp.concatenate(u, axis=1).astype(BF16)
    x = jnp.dot(a, w_ref[0], preferred_element_type=F32)
    h_re, h_im = hr_ref[...], hi_ref[...]
    lhs = jnp.concatenate([a, h_re.astype(BF16), h_im.astype(BF16)], axis=1)
    y = jnp.dot(lhs, mv_ref[0], preferred_element_type=F32)
    _s5_emit(z_ref, y, u, d_ref[...], 0, n_chunks)
    fr_ref[...] = a_re * h_re - a_im * h_im + x[:, :sw]
    fi_ref[...] = a_re * h_im + a_im * h_re + x[:, sw:]


def s5_core(qkvu, coef, d_skip, ssm_width, h0, rows=256):
    w_all, mv_all, al = coef
    nb, aw, sw2 = w_all.shape
    sw = sw2 // 2
    t = qkvu.shape[0]
    n_chunks = t // S5_CHUNK
    assert t % S5_CHUNK == 0 and nb * LANES == ssm_width
    u_blk0 = (qkvu.shape[1] - ssm_width) // LANES
    in_specs = [pl.BlockSpec((t, LANES), lambda b: (0, u_blk0 + b)),
                pl.BlockSpec((1, LANES), lambda b: (0, b)),
                pl.BlockSpec((1, aw, sw2), lambda b: (b, 0, 0)),
                pl.BlockSpec((1, aw + sw2, aw), lambda b: (b, 0, 0)),
                pl.BlockSpec((1, SUBLANES, sw2), lambda b: (b, 0, 0))]
    z_spec = pl.BlockSpec((t, LANES), lambda b: (0, b))
    z_shape = jax.ShapeDtypeStruct((t, ssm_width), F32)
    args = (qkvu, d_skip.reshape(1, ssm_width), w_all, mv_all, al)
    if h0 is None:
        rows = min(rows, n_chunks)
        assert n_chunks % rows == 0 and rows % SUBLANES == 0
        st = pl.BlockSpec((SUBLANES, sw), lambda b: (0, b))
        z, f_re, f_im = pl.pallas_call(
            functools.partial(_s5_prompt_body, n_chunks=n_chunks, rows=rows),
            grid=(nb,),
            in_specs=in_specs,
            out_specs=[z_spec, st, st],
            out_shape=[z_shape] + [jax.ShapeDtypeStruct((SUBLANES, nb * sw), F32)] * 2,
            scratch_shapes=[pltpu.VMEM((rows, sw2), F32), pltpu.VMEM((rows, sw2), F32)],
            compiler_params=_cparams("arbitrary"),
            name="s5_prompt",
        )(*args)
        return z, f_re[:1], f_im[:1]
    assert h0[0].shape == (n_chunks, nb * sw)
    st = pl.BlockSpec((n_chunks, sw), lambda b: (0, b))
    return pl.pallas_call(
        functools.partial(_s5_sample_body, n_chunks=n_chunks),
        grid=(nb,),
        in_specs=in_specs + [st, st],
        out_specs=[z_spec, st, st],
        out_shape=[z_shape] + [jax.ShapeDtypeStruct((n_chunks, nb * sw), F32)] * 2,
        compiler_params=_cparams("arbitrary"),
        name="s5_sample",
    )(*args, *h0)


def _s5_glu_body(z_ref, w_ref, b_ref, o_ref, zb_ref, *, tn):
    j = pl.program_id(1)

    @pl.when(j == 0)
    def _():
        zb_ref[...] = z_ref[...].astype(BF16)

    col = pl.multiple_of(j * tn, tn)
    lin = jnp.dot(zb_ref[...], w_ref[...], preferred_element_type=F32) + b_ref[...]
    o_ref[...] = z_ref[:, pl.ds(col, tn)] * jax.nn.sigmoid(lin)


def s5_glu(z, w_glu, b_glu, tm=512, tn=512):
    t, ws = z.shape
    tm = min(tm, t)
    return pl.pallas_call(
        functools.partial(_s5_glu_body, tn=tn),
        grid=(t // tm, ws // tn),
        in_specs=[pl.BlockSpec((tm, ws), lambda i, j: (i, 0)),
                  pl.BlockSpec((ws, tn), lambda i, j: (0, j)),
                  pl.BlockSpec((1, tn), lambda i, j: (0, j))],
        out_specs=pl.BlockSpec((tm, tn), lambda i, j: (i, j)),
        out_shape=jax.ShapeDtypeStruct((t, ws), F32),
        scratch_shapes=[pltpu.VMEM((tm, ws), BF16)],
        compiler_params=_cparams("arbitrary", "arbitrary"),
        name="s5_glu",
    )(z, w_glu, b_glu.reshape(1, ws))


def _layer(x, mod, attend, h0, lw, coef, dims):
    d_model, attn_w, ssm_w = dims
    h = norm_mod(x, lw["g_norm1"], mod, 1, 0)
    qkvu = inproj(h, lw["w_in"], lw["layer"], lw["g_q"], lw["g_k"], attn_w)
    o_attn = attend(qkvu)
    z, f_re, f_im = s5_core(qkvu, coef, lw["s5_d"], ssm_w, h0)
    o_ssm = s5_glu(z, lw["w_glu"], lw["b_glu"])
    mix = rms2(o_attn, o_ssm, lw["g_attn_out"], lw["g_ssm_out"])
    x1 = matmul_res(mix, lw["w_out"], x, mod, 2, tm=1024, tn=512)
    h2 = norm_mod(x1, lw["g_norm2"], mod, 4, 3)
    hid = gateup(h2, lw["w_gate"], lw["w_up"], lw["layer"])
    y = matmul_res(hid, lw["w_down"], x1, mod, 5, tm=512, tn=256)
    k = qkvu[:, attn_w:2 * attn_w]
    v = qkvu[:, 2 * attn_w:3 * attn_w]
    return y, k, v, f_re, f_im


def kernel(x_prompt, x_sample, cache_k, cache_v, state_s5_re, state_s5_im, page_table, c_prompt, c_sample, w_ada, b_ada, g_norm1, w_in, g_q, g_k, s5_lam_re, s5_lam_im, s5_log_step, s5_b_re, s5_b_im, s5_c_re, s5_c_im, s5_d, w_glu, b_glu, g_attn_out, g_ssm_out, w_out, g_norm2, w_gate, w_up, w_down):
    depth = w_ada.shape[0]
    batch, seq, d_model = x_prompt.shape
    dec_batch, dec_seq, _ = x_sample.shape
    n_pool, n_heads, head_dim = cache_k.shape[1], cache_k.shape[3], cache_k.shape[4]
    assert head_dim == HEAD_DIM and cache_k.shape[2] == PAGE_SIZE and batch == 1
    assert dec_seq == S5_CHUNK
    attn_w = n_heads * head_dim
    ssm_w = d_model - attn_w
    n_groups, n_state = state_s5_re.shape[2], state_s5_re.shape[3]
    past_len = page_table.shape[1] * PAGE_SIZE
    dims = (d_model, attn_w, ssm_w)
    ck = cache_k.reshape(depth, n_pool, PAGE_SIZE * n_heads, head_dim)
    cv = cache_v.reshape(depth, n_pool, PAGE_SIZE * n_heads, head_dim)

    y_p = x_prompt.reshape(batch * seq, d_model)
    y_s = x_sample.reshape(dec_batch * dec_seq, d_model)
    outs = [[] for _ in range(8)]
    for l in range(depth):
        lw = dict(layer=l, g_norm1=g_norm1[l], w_in=w_in, g_q=g_q[l], g_k=g_k[l],
                  s5_d=s5_d[l], w_glu=w_glu[l].astype(BF16), b_glu=b_glu[l],
                  g_attn_out=g_attn_out[l], g_ssm_out=g_ssm_out[l], w_out=w_out[l].astype(BF16),
                  g_norm2=g_norm2[l], w_gate=w_gate, w_up=w_up,
                  w_down=w_down[l].astype(BF16))
        coef = s5_coef(s5_lam_re[l], s5_lam_im[l], s5_log_step[l], s5_b_re[l], s5_b_im[l],
                       s5_c_re[l], s5_c_im[l])
        pad = (-(dec_batch + batch)) % SUBLANES
        c_all = jnp.concatenate([c_sample, c_prompt, jnp.zeros((pad, d_model), F32)], axis=0)
        mod = adaln(c_all, w_ada[l], b_ada[l])
        mod_p = mod[dec_batch:dec_batch + 1]
        mod_s = jnp.repeat(mod[:dec_batch], dec_seq, axis=0)
        h0 = (state_s5_re[l].reshape(dec_batch, n_groups * n_state),
              state_s5_im[l].reshape(dec_batch, n_groups * n_state))

        y_p, kp, vp, hrp, hip = _layer(
            y_p, mod_p, functools.partial(moba_prompt, n_heads=n_heads), None, lw, coef, dims)
        y_s, ks, vs, hrs, his = _layer(
            y_s, mod_s,
            functools.partial(moba_sample, cache_k=ck, cache_v=cv, layer=l, page_table=page_table,
                              n_heads=n_heads, t_new=dec_seq, past_len=past_len),
            h0, lw, coef, dims)
        vals = (kp.reshape(batch, seq, n_heads, head_dim), vp.reshape(batch, seq, n_heads, head_dim),
                hrp.reshape(batch, n_groups, n_state), hip.reshape(batch, n_groups, n_state),
                ks.reshape(dec_batch, dec_seq, n_heads, head_dim),
                vs.reshape(dec_batch, dec_seq, n_heads, head_dim),
                hrs.reshape(dec_batch, n_groups, n_state), his.reshape(dec_batch, n_groups, n_state))
        for o, val in zip(outs, vals):
            o.append(val)
    return (y_p.reshape(batch, seq, d_model), y_s.reshape(dec_batch, dec_seq, d_model),
            *[jnp.stack(o) for o in outs])
```
